```python
import jax, jax.numpy as jnp
from jax import lax
import numpy as np

D_MODEL = 1024
BATCH = 16
SEQ = 4096
DEPTH = 4

N_A_LAYERS = DEPTH // 2
N_B_LAYERS = DEPTH - N_A_LAYERS
N_DENSE = (DEPTH + 1) // 2
N_MOE = DEPTH // 2

D_RNN = D_MODEL
RNN_BLOCKS = 8
RNN_BLOCK_W = D_RNN // RNN_BLOCKS
CONV_W = 4
LRU_C = 8.0

HEAD_DIM = 64
N_Q_HEADS = D_MODEL // HEAD_DIM
N_KV_HEADS = 4
GROUP = N_Q_HEADS // N_KV_HEADS
WINDOW = 128
BLOCK = 128
ROPE_THETA = 10000.0

D_FF = 3 * D_MODEL
N_EXPERTS = 8
TOP_K = 2
D_FF_EXPERT = 7 * D_MODEL // 2

EPS = 1e-6
F32 = jnp.float32

kernel_name = "hybrid_rglru_yoco_swa_sink_moe"


def rms_norm(x, g):
    x32 = x.astype(F32)
    y = x32 * lax.rsqrt(jnp.mean(x32 * x32, axis=-1, keepdims=True) + EPS)
    return (y * g.astype(F32)).astype(x.dtype)


def modulate(x, g, shift, scale):
    return rms_norm(x, g) * (1 + scale[:, None, :]) + shift[:, None, :]


def rope_tables(positions, dtype):
    inv = ROPE_THETA ** (-jnp.arange(0, HEAD_DIM, 2, dtype=F32) / HEAD_DIM)
    ang = positions.astype(F32)[..., None] * inv
    return jnp.cos(ang)[:, :, None, :].astype(dtype), jnp.sin(ang)[:, :, None, :].astype(dtype)


def rotary(x, cos, sin):
    x1, x2 = jnp.split(x, 2, axis=-1)
    return jnp.concatenate([x1 * cos - x2 * sin, x2 * cos + x1 * sin], axis=-1)


def rglru_mixer(h, w_in, conv_w, conv_b, gx_w, gx_b, ga_w, ga_b, lam, w_out):
    bsz, seq, _ = h.shape
    xb, yb = jnp.split(h @ w_in, 2, axis=-1)
    xp = jnp.pad(xb, ((0, 0), (CONV_W - 1, 0), (0, 0)))
    xc = conv_b + xp[:, 0:seq] * conv_w[0]
    for k in range(1, CONV_W):
        xc = xc + xp[:, k:k + seq] * conv_w[k]
    xblk = xc.reshape(bsz, seq, RNN_BLOCKS, RNN_BLOCK_W)
    r_gate = jax.nn.sigmoid((jnp.einsum('bsnc,ncd->bsnd', xblk, ga_w).reshape(bsz, seq, D_RNN) + ga_b).astype(F32))
    i_gate = jax.nn.sigmoid((jnp.einsum('bsnc,ncd->bsnd', xblk, gx_w).reshape(bsz, seq, D_RNN) + gx_b).astype(F32))
    log_a = -LRU_C * r_gate * jax.nn.softplus(-lam.astype(F32))
    a = jnp.exp(log_a)
    b = jnp.sqrt(-jnp.expm1(2.0 * log_a)) * (i_gate * xc.astype(F32))

    def combine(left, right):
        a1, b1 = left
        a2, b2 = right
        return a1 * a2, a2 * b1 + b2

    _, hs = lax.associative_scan(combine, (a, b), axis=1)
    return (hs.astype(h.dtype) * jax.nn.gelu(yb)) @ w_out


def band_mask(nblk):
    n = jnp.arange(nblk)[:, None, None]
    i = jnp.arange(BLOCK)[None, :, None]
    j = jnp.arange(2 * BLOCK)[None, None, :]
    s = n * BLOCK - BLOCK + j
    d = n * BLOCK + i - s
    return (d >= 0) & (d < WINDOW) & (s >= 0)


def shared_kv(x, c_act, kv_norm, kv_ada_w, kv_ada_b, w_kv, k_norm, cos, sin):
    bsz, seq, _ = x.shape
    shift, scale = jnp.split(c_act @ kv_ada_w + kv_ada_b, 2, axis=-1)
    h = modulate(x, kv_norm, shift, scale)
    k, v = jnp.split(h @ w_kv, 2, axis=-1)
    k = k.reshape(bsz, seq, N_KV_HEADS, HEAD_DIM)
    v = v.reshape(bsz, seq, N_KV_HEADS, HEAD_DIM)
    k = rotary(rms_norm(k, k_norm), cos, sin)
    nblk = seq // BLOCK

    def to_windows(t):
        tb = t.reshape(bsz, nblk, BLOCK, N_KV_HEADS, HEAD_DIM).transpose(1, 0, 2, 3, 4)
        prev = jnp.concatenate([jnp.zeros_like(tb[:1]), tb[:-1]], axis=0)
        return jnp.concatenate([prev, tb], axis=2)

    return to_windows(k), to_windows(v)


def swa_sink_attention(h, w_q, q_norm, sinks, w_o, k_win, v_win, mask, cos, sin):
    bsz, seq, _ = h.shape
    nblk = seq // BLOCK
    q = (h @ w_q).reshape(bsz, seq, N_Q_HEADS, HEAD_DIM)
    q = rotary(rms_norm(q, q_norm), cos, sin)
    qb = q.reshape(bsz, nblk, BLOCK, N_KV_HEADS, GROUP, HEAD_DIM).transpose(1, 0, 2, 3, 4, 5)
    sink = sinks.astype(F32).reshape(N_KV_HEADS, GROUP)
    scale = HEAD_DIM ** -0.5

    def block_attn(args):
        qi, ki, vi, mi = args
        s = jnp.einsum('bqkgd,bskd->bkgqs', qi, ki).astype(F32) * scale
        s = jnp.where(mi[None, None, None], s, -jnp.inf)
        sink_col = jnp.broadcast_to(sink[None, :, :, None, None], s.shape[:-1] + (1,))
        p = jax.nn.softmax(jnp.concatenate([s, sink_col], axis=-1), axis=-1)[..., :-1]
        return jnp.einsum('bkgqs,bskd->bqkgd', p.astype(vi.dtype), vi)

    o = lax.map(block_attn, (qb, k_win, v_win, mask))
    o = o.transpose(1, 0, 2, 3, 4, 5).reshape(bsz, seq, N_Q_HEADS * HEAD_DIM)
    return o @ w_o


def swiglu(h, w_gu, w_down):
    g, u = jnp.split(h @ w_gu, 2, axis=-1)
    return (jax.nn.silu(g) * u) @ w_down


def moe_swiglu(h, w_router, w_gu, w_down):
    bsz, seq, d = h.shape
    t = h.reshape(-1, d)
    logits = (t @ w_router).astype(F32)
    top_v, top_i = lax.top_k(logits, TOP_K)
    top_w = jax.nn.softmax(top_v, axis=-1)
    gates = jnp.sum(jax.nn.one_hot(top_i, N_EXPERTS, dtype=F32) * top_w[..., None], axis=1)
    out = jnp.zeros_like(t)
    for e in range(N_EXPERTS):
        out = out + gates[:, e:e + 1].astype(t.dtype) * swiglu(t, w_gu[e], w_down[e])
    return out.reshape(bsz, seq, d)


def setup_inputs(seed: int = 0) -> dict:
    key = jax.random.key(seed)
    ks = iter(jax.random.split(key, 40))
    nrm = lambda shape, s: jax.random.normal(next(ks), shape, F32) * s
    gain = lambda shape: 1.0 + 0.05 * jax.random.normal(next(ks), shape, F32)
    D = D_MODEL
    x = nrm((BATCH, SEQ, D), 1.0)
    c = nrm((BATCH, D), 1.0)
    offs = jax.random.randint(next(ks), (BATCH, 1), 0, 1024, dtype=jnp.int32)
    positions = (offs + jnp.arange(SEQ, dtype=jnp.int32)[None, :]).astype(jnp.int32)
    ada_w = nrm((DEPTH, D, 6 * D), 0.1 * D ** -0.5)
    ada_b = nrm((DEPTH, 6 * D), 0.02)
    mix_norm = gain((DEPTH, D))
    ffn_norm = gain((DEPTH, D))
    a_w_in = nrm((N_A_LAYERS, D, 2 * D_RNN), D ** -0.5)
    a_conv_w = nrm((N_A_LAYERS, CONV_W, D_RNN), CONV_W ** -0.5)
    a_conv_b = nrm((N_A_LAYERS, D_RNN), 0.02)
    a_gx_w = nrm((N_A_LAYERS, RNN_BLOCKS, RNN_BLOCK_W, RNN_BLOCK_W), RNN_BLOCK_W ** -0.5)
    a_gx_b = nrm((N_A_LAYERS, D_RNN), 0.02)
    a_ga_w = nrm((N_A_LAYERS, RNN_BLOCKS, RNN_BLOCK_W, RNN_BLOCK_W), RNN_BLOCK_W ** -0.5)
    a_ga_b = nrm((N_A_LAYERS, D_RNN), 0.02)
    u = jax.random.uniform(next(ks), (N_A_LAYERS, D_RNN), F32, 0.9, 0.999)
    sig = u ** (1.0 / LRU_C)
    a_lambda = jnp.log(sig) - jnp.log1p(-sig)
    a_w_out = nrm((N_A_LAYERS, D_RNN, D), D_RNN ** -0.5)
    kv_norm = gain((D,))
    kv_ada_w = nrm((D, 2 * D), 0.1 * D ** -0.5)
    kv_ada_b = nrm((2 * D,), 0.02)
    w_kv = nrm((D, 2 * N_KV_HEADS * HEAD_DIM), D ** -0.5)
    k_norm = gain((HEAD_DIM,))
    b_w_q = nrm((N_B_LAYERS, D, N_Q_HEADS * HEAD_DIM), D ** -0.5)
    b_q_norm = gain((N_B_LAYERS, HEAD_DIM))
    b_sinks = nrm((N_B_LAYERS, N_Q_HEADS), 1.0)
    b_w_o = nrm((N_B_LAYERS, N_Q_HEADS * HEAD_DIM, D), (N_Q_HEADS * HEAD_DIM) ** -0.5)
    f_w_gu = nrm((N_DENSE, D, 2 * D_FF), D ** -0.5)
    f_w_down = nrm((N_DENSE, D_FF, D), D_FF ** -0.5)
    m_router = nrm((N_MOE, D, N_EXPERTS), D ** -0.5)
    m_w_gu = nrm((N_MOE, N_EXPERTS, D, 2 * D_FF_EXPERT), D ** -0.5)
    m_w_down = nrm((N_MOE, N_EXPERTS, D_FF_EXPERT, D), D_FF_EXPERT ** -0.5)
    return {"x": x, "c": c, "positions": positions, "ada_w": ada_w, "ada_b": ada_b,
            "mix_norm": mix_norm, "ffn_norm": ffn_norm, "a_w_in": a_w_in, "a_conv_w": a_conv_w,
            "a_conv_b": a_conv_b, "a_gx_w": a_gx_w, "a_gx_b": a_gx_b, "a_ga_w": a_ga_w,
            "a_ga_b": a_ga_b, "a_lambda": a_lambda, "a_w_out": a_w_out, "kv_norm": kv_norm,
            "kv_ada_w": kv_ada_w, "kv_ada_b": kv_ada_b, "w_kv": w_kv, "k_norm": k_norm,
            "b_w_q": b_w_q, "b_q_norm": b_q_norm, "b_sinks": b_sinks, "b_w_o": b_w_o,
            "f_w_gu": f_w_gu, "f_w_down": f_w_down, "m_router": m_router, "m_w_gu": m_w_gu,
            "m_w_down": m_w_down}


def reference(x, c, positions, ada_w, ada_b, mix_norm, ffn_norm, a_w_in, a_conv_w, a_conv_b,
              a_gx_w, a_gx_b, a_ga_w, a_ga_b, a_lambda, a_w_out, kv_norm, kv_ada_w, kv_ada_b,
              w_kv, k_norm, b_w_q, b_q_norm, b_sinks, b_w_o, f_w_gu, f_w_down, m_router,
              m_w_gu, m_w_down):
    cos, sin = rope_tables(positions, x.dtype)
    c_act = jax.nn.silu(c)
    mask = band_mask(x.shape[1] // BLOCK)
    k_win = None
    v_win = None
    for l in range(DEPTH):
        mod = c_act @ ada_w[l] + ada_b[l]
        sh1, sc1, g1, sh2, sc2, g2 = jnp.split(mod, 6, axis=-1)
        if l < N_A_LAYERS:
            h = modulate(x, mix_norm[l], sh1, sc1)
            y = rglru_mixer(h, a_w_in[l], a_conv_w[l], a_conv_b[l], a_gx_w[l], a_gx_b[l],
                            a_ga_w[l], a_ga_b[l], a_lambda[l], a_w_out[l])
        else:
            if l == N_A_LAYERS:
                k_win, v_win = shared_kv(x, c_act, kv_norm, kv_ada_w, kv_ada_b, w_kv, k_norm, cos, sin)
            bi = l - N_A_LAYERS
            h = modulate(x, mix_norm[l], sh1, sc1)
            y = swa_sink_attention(h, b_w_q[bi], b_q_norm[bi], b_sinks[bi], b_w_o[bi],
                                   k_win, v_win, mask, cos, sin)
        x = x + (1 + g1[:, None, :]) * y
        h = modulate(x, ffn_norm[l], sh2, sc2)
        if l % 2 == 0:
            y = swiglu(h, f_w_gu[l // 2], f_w_down[l // 2])
        else:
            y = moe_swiglu(h, m_router[l // 2], m_w_gu[l // 2], m_w_down[l // 2])
        x = x + (1 + g2[:, None, :]) * y
    return x
```

```python
import functools

import jax
import jax.numpy as jnp
from jax import lax
from jax.experimental import pallas as pl
from jax.experimental.pallas import tpu as pltpu

F32 = jnp.float32
BF16 = jnp.bfloat16

EPS = 1e-6
HEAD_DIM = 64
N_KV_HEADS = 4
GROUP = 4
BLOCK = 128
ROPE_THETA = 10000.0
LRU_C = 8.0
RNN_BLOCK_W = 128
SUBLANES = 8
LANES = 128
VMEM_LIMIT = 56 * 1024 * 1024


def _cparams(*sem):
    return pltpu.CompilerParams(dimension_semantics=sem, vmem_limit_bytes=VMEM_LIMIT)


def _modulate(x, g, shift, scale):
    ms = jnp.mean(x * x, axis=-1, keepdims=True)
    return (x * lax.rsqrt(ms + EPS)) * g * (1.0 + scale) + shift


def _silu(v):
    return v * jax.nn.sigmoid(v)


def _cond_linear_kernel(c_ref, w_ref, b_ref, o_ref):
    ca = _silu(c_ref[...]).astype(BF16)
    o_ref[...] = jnp.dot(ca, w_ref[...].astype(BF16), preferred_element_type=F32) + b_ref[...]


def _cond_linear(c, w, b, tn=1024):
    nl, d, n = w.shape
    bsz = c.shape[0]
    return pl.pallas_call(
        _cond_linear_kernel,
        grid=(nl, n // tn),
        in_specs=[pl.BlockSpec((bsz, d), lambda l, j: (0, 0)),
                  pl.BlockSpec((None, d, tn), lambda l, j: (l, 0, j)),
                  pl.BlockSpec((None, 1, tn), lambda l, j: (l, 0, j))],
        out_specs=pl.BlockSpec((None, bsz, tn), lambda l, j: (l, 0, j)),
        out_shape=jax.ShapeDtypeStruct((nl, bsz, n), F32),
        compiler_params=_cparams("parallel", "parallel"),
        name="cond_linear",
    )(c, w, b.reshape(nl, 1, n))


def _rope_kernel(pos_ref, inv_ref, cos_ref, sin_ref):
    ang = inv_ref[...] * pos_ref[...].astype(F32)
    cos_ref[...] = jnp.cos(ang)
    sin_ref[...] = jnp.sin(ang)


def _rope_tables(positions):
    bsz, seq = positions.shape
    half = HEAD_DIM // 2
    inv = ROPE_THETA ** (-jnp.arange(0, HEAD_DIM, 2, dtype=F32) / HEAD_DIM)
    out = jax.ShapeDtypeStruct((bsz, half, seq), F32)
    return pl.pallas_call(
        _rope_kernel,
        grid=(bsz,),
        in_specs=[pl.BlockSpec((None, 1, seq), lambda b: (b, 0, 0)),
                  pl.BlockSpec((half, 1), lambda b: (0, 0))],
        out_specs=[pl.BlockSpec((None, half, seq), lambda b: (b, 0, 0))] * 2,
        out_shape=[out, out],
        compiler_params=_cparams("parallel"),
        name="rope_tables",
    )(positions.reshape(bsz, 1, seq), inv.reshape(half, 1))


def _norm_rotate_heads(t, n_heads, norm_col, cos_t, sin_t, post_scale):
    half = HEAD_DIM // 2
    outs = []
    for hh in range(n_heads):
        th = t[hh * HEAD_DIM:(hh + 1) * HEAD_DIM, :]
        ms = jnp.mean(th * th, axis=0, keepdims=True)
        th = th * lax.rsqrt(ms + EPS) * norm_col
        x1, x2 = th[:half, :], th[half:, :]
        r = jnp.concatenate([x1 * cos_t - x2 * sin_t, x2 * cos_t + x1 * sin_t], axis=0)
        outs.append(r * post_scale if post_scale != 1.0 else r)
    return outs


def _rglru_kernel(x_ref, sh_ref, sc_ref, gt_ref, nrm_ref, win_ref, cw_ref, cb_ref, wg_ref,
                  gab_ref, gxb_ref, lam_ref, wout_ref, o_ref, xbuf, a_s, b_s, hst):
    ts, d = x_ref.shape
    tail = SUBLANES

    @pl.when(pl.program_id(1) == 0)
    def _():
        xbuf[0:tail, :] = jnp.zeros((tail, d), F32)
        hst[...] = jnp.zeros_like(hst)

    x = x_ref[...]
    h = _modulate(x, nrm_ref[...], sh_ref[...], sc_ref[...]).astype(BF16)
    xy = jnp.dot(h, win_ref[...], preferred_element_type=F32)
    xb, yb = xy[:, :d], xy[:, d:]

    xbuf[tail:tail + ts, :] = xb
    cw = cw_ref[...]
    nk = cw.shape[0]
    xc = cb_ref[...] + xb * cw[nk - 1:nk, :]
    for j in range(1, nk):
        xc = xc + xbuf[tail - j:tail - j + ts, :] * cw[nk - 1 - j:nk - j, :]
    xbuf[0:tail, :] = xb[ts - tail:ts, :]

    xcb = xc.astype(BF16)
    nb = d // RNN_BLOCK_W
    zs = [jnp.dot(xcb[:, n * RNN_BLOCK_W:(n + 1) * RNN_BLOCK_W], wg_ref[n],
                  preferred_element_type=F32) for n in range(nb)]
    r_gate = jax.nn.sigmoid(jnp.concatenate([z[:, :RNN_BLOCK_W] for z in zs], axis=1) + gab_ref[...])
    i_gate = jax.nn.sigmoid(jnp.concatenate([z[:, RNN_BLOCK_W:] for z in zs], axis=1) + gxb_ref[...])
    log_a = (-LRU_C) * r_gate * jax.nn.softplus(-lam_ref[...])
    a = jnp.exp(log_a)
    a_s[...] = a
    b_s[...] = jnp.sqrt(1.0 - a * a) * (i_gate * xc)

    rowi = lax.broadcasted_iota(jnp.int32, (SUBLANES, d), 0)

    def group(ci, hprev):
        r0 = pl.multiple_of(ci * SUBLANES, SUBLANES)
        a8 = a_s[pl.ds(r0, SUBLANES), :]
        b8 = b_s[pl.ds(r0, SUBLANES), :]
        for dist in (1, 2, 4):
            keep = rowi >= dist
            a_sh = pltpu.roll(a8, dist, 0)
            b_sh = pltpu.roll(b8, dist, 0)
            b8 = jnp.where(keep, b8 + a8 * b_sh, b8)
            a8 = jnp.where(keep, a8 * a_sh, a8)
        hs8 = b8 + a8 * hprev
        b_s[pl.ds(r0, SUBLANES), :] = hs8
        return jnp.broadcast_to(hs8[SUBLANES - 1:SUBLANES, :], (SUBLANES, d))

    hst[...] = lax.fori_loop(0, ts // SUBLANES, group, hst[...])

    o = (b_s[...] * jax.nn.gelu(yb, approximate=True)).astype(BF16)
    y = jnp.dot(o, wout_ref[...], preferred_element_type=F32)
    o_ref[...] = x + (1.0 + gt_ref[...]) * y


def _rglru_layer(x, mod, nrm, w_in, conv_w, conv_b, wg, ga_b, gx_b, lam, w_out, ts):
    bsz, seq, d = x.shape
    row = lambda k: pl.BlockSpec((None, 1, d), lambda b, s, k=k: (b, 0, k))
    full = lambda a: pl.BlockSpec(a.shape, lambda b, s, n=a.ndim: (0,) * n)
    tile = pl.BlockSpec((None, ts, d), lambda b, s: (b, s, 0))
    return pl.pallas_call(
        _rglru_kernel,
        grid=(bsz, seq // ts),
        in_specs=[tile, row(0), row(1), row(2), full(nrm), full(w_in), full(conv_w), full(conv_b),
                  full(wg), full(ga_b), full(gx_b), full(lam), full(w_out)],
        out_specs=tile,
        out_shape=jax.ShapeDtypeStruct(x.shape, F32),
        scratch_shapes=[pltpu.VMEM((ts + SUBLANES, d), F32), pltpu.VMEM((ts, d), F32),
                        pltpu.VMEM((ts, d), F32), pltpu.VMEM((SUBLANES, d), F32)],
        compiler_params=_cparams("parallel", "arbitrary"),
        name="rglru_layer",
    )(x, mod, mod, mod, nrm, w_in, conv_w, conv_b, wg, ga_b, gx_b, lam, w_out)


def _kv_kernel(x_ref, sh_ref, sc_ref, nrm_ref, wkvt_ref, kn_ref, cos_ref, sin_ref, k_ref, vt_ref):
    h = _modulate(x_ref[...], nrm_ref[...], sh_ref[...], sc_ref[...]).astype(BF16)
    kvt = lax.dot_general(wkvt_ref[...], h, (((1,), (1,)), ((), ())), preferred_element_type=F32)
    nk = N_KV_HEADS * HEAD_DIM
    heads = _norm_rotate_heads(kvt[:nk, :], N_KV_HEADS, kn_ref[...], cos_ref[...], sin_ref[...], 1.0)
    k_ref[...] = jnp.concatenate(heads, axis=0).T.astype(BF16)
    vt_ref[...] = kvt[nk:, :].astype(BF16)


def _shared_kv(x, kvmod, nrm, w_kv_t, kn_col, cos_t, sin_t, tm):
    bsz, seq, d = x.shape
    nk = N_KV_HEADS * HEAD_DIM
    half = HEAD_DIM // 2
    row = lambda k: pl.BlockSpec((None, 1, d), lambda b, s, k=k: (b, 0, k))
    full = lambda a: pl.BlockSpec(a.shape, lambda b, s, n=a.ndim: (0,) * n)
    rope = pl.BlockSpec((None, half, tm), lambda b, s: (b, 0, s))
    return pl.pallas_call(
        _kv_kernel,
        grid=(bsz, seq // tm),
        in_specs=[pl.BlockSpec((None, tm, d), lambda b, s: (b, s, 0)), row(0), row(1), full(nrm),
                  full(w_kv_t), full(kn_col), rope, rope],
        out_specs=[pl.BlockSpec((None, tm, nk), lambda b, s: (b, s, 0)),
                   pl.BlockSpec((None, nk, tm), lambda b, s: (b, 0, s))],
        out_shape=[jax.ShapeDtypeStruct((bsz, seq, nk), BF16), jax.ShapeDtypeStruct((bsz, nk, seq), BF16)],
        compiler_params=_cparams("parallel", "parallel"),
        name="shared_kv",
    )(x, kvmod, kvmod, nrm, w_kv_t, kn_col, cos_t, sin_t)


def _attn_kernel(x_ref, sh_ref, sc_ref, gt_ref, nrm_ref, wqt_ref, qn_ref, cos_ref, sin_ref,
                 kc_ref, kp_ref, vc_ref, vp_ref, sink_ref, wo_ref, o_ref, ot_s):
    tq, d = x_ref.shape
    nqb = tq // BLOCK
    n_heads = d // HEAD_DIM
    gw = GROUP * BLOCK
    x = x_ref[...]
    h = _modulate(x, nrm_ref[...], sh_ref[...], sc_ref[...]).astype(BF16)
    qt = lax.dot_general(wqt_ref[...], h, (((1,), (1,)), ((), ())), preferred_element_type=F32)
    heads = _norm_rotate_heads(qt, n_heads, qn_ref[...], cos_ref[...], sin_ref[...], HEAD_DIM ** -0.5)
    heads = [q.astype(BF16) for q in heads]

    kj = lax.broadcasted_iota(jnp.int32, (2 * BLOCK, gw), 0)
    qi = lax.broadcasted_iota(jnp.int32, (2 * BLOCK, gw), 1) % BLOCK
    band = (kj > qi) & (kj <= qi + BLOCK)
    first_tile = pl.program_id(1) == 0
    zero_q = jnp.zeros((HEAD_DIM, gw), BF16)

    for jb in range(nqb):
        lo, hi = jb * BLOCK, (jb + 1) * BLOCK
        if jb == 0:
            k_prev, v_prev = kp_ref[...], vp_ref[...]
            mask = band & (jnp.logical_not(first_tile) | (kj >= BLOCK))
        else:
            k_prev, v_prev = kc_ref[lo - BLOCK:lo, :], vc_ref[:, lo - BLOCK:lo]
            mask = band
        k2 = jnp.concatenate([k_prev, kc_ref[lo:hi, :]], axis=0)
        v2t = jnp.concatenate([v_prev, vc_ref[:, lo:hi]], axis=1)
        for kh in range(N_KV_HEADS):
            qblk = jnp.concatenate([heads[kh * GROUP + g][:, lo:hi] for g in range(GROUP)], axis=1)
            qbd = jnp.concatenate([zero_q] * kh + [qblk] + [zero_q] * (N_KV_HEADS - 1 - kh), axis=0)
            s = jnp.dot(k2, qbd, preferred_element_type=F32)
            s = jnp.where(mask, s, -jnp.inf)
            sk = sink_ref[:, kh * gw:(kh + 1) * gw]
            m = jnp.maximum(jnp.max(s, axis=0, keepdims=True), sk)
            e = jnp.exp(s - m)
            den = jnp.sum(e, axis=0, keepdims=True) + jnp.exp(sk - m)
            p = (e * (1.0 / den)).astype(BF16)
            ot = jnp.dot(v2t[kh * HEAD_DIM:(kh + 1) * HEAD_DIM, :], p, preferred_element_type=F32)
            for g in range(GROUP):
                r0 = (kh * GROUP + g) * HEAD_DIM
                ot_s[r0:r0 + HEAD_DIM, lo:hi] = ot[:, g * BLOCK:(g + 1) * BLOCK]

    o = ot_s[...].T.astype(BF16)
    y = jnp.dot(o, wo_ref[...], preferred_element_type=F32)
    o_ref[...] = x + (1.0 + gt_ref[...]) * y


def _attn_layer(x, mod, nrm, w_q_t, qn_col, cos_t, sin_t, k, v_t, sink_row, w_o, tq):
    bsz, seq, d = x.shape
    nk = N_KV_HEADS * HEAD_DIM
    half = HEAD_DIM // 2
    nqb = tq // BLOCK
    row = lambda kk: pl.BlockSpec((None, 1, d), lambda b, s, kk=kk: (b, 0, kk))
    full = lambda a: pl.BlockSpec(a.shape, lambda b, s, n=a.ndim: (0,) * n)
    tile = pl.BlockSpec((None, tq, d), lambda b, s: (b, s, 0))
    rope = pl.BlockSpec((None, half, tq), lambda b, s: (b, 0, s))
    prev = lambda s: jnp.maximum(s * nqb - 1, 0)
    return pl.pallas_call(
        _attn_kernel,
        grid=(bsz, seq // tq),
        in_specs=[tile, row(0), row(1), row(2), full(nrm), full(w_q_t), full(qn_col), rope, rope,
                  pl.BlockSpec((None, tq, nk), lambda b, s: (b, s, 0)),
                  pl.BlockSpec((None, BLOCK, nk), lambda b, s: (b, prev(s), 0)),
                  pl.BlockSpec((None, nk, tq), lambda b, s: (b, 0, s)),
                  pl.BlockSpec((None, nk, BLOCK), lambda b, s: (b, 0, prev(s))),
                  full(sink_row), full(w_o)],
        out_specs=tile,
        out_shape=jax.ShapeDtypeStruct(x.shape, F32),
        scratch_shapes=[pltpu.VMEM((d, tq), F32)],
        compiler_params=_cparams("parallel", "parallel"),
        name="attn_layer",
    )(x, mod, mod, mod, nrm, w_q_t, qn_col, cos_t, sin_t, k, k, v_t, v_t, sink_row, w_o)


def _ffn_kernel(x_ref, sh_ref, sc_ref, gt_ref, nrm_ref, wg_ref, wu_ref, wd_ref, o_ref, h_s, acc):
    j = pl.program_id(1)

    @pl.when(j == 0)
    def _():
        h_s[...] = _modulate(x_ref[...], nrm_ref[...], sh_ref[...], sc_ref[...]).astype(BF16)
        acc[...] = jnp.zeros_like(acc)

    h = h_s[...]
    g = jnp.dot(h, wg_ref[...], preferred_element_type=F32)
    u = jnp.dot(h, wu_ref[...], preferred_element_type=F32)
    acc[...] += jnp.dot((_silu(g) * u).astype(BF16), wd_ref[...], preferred_element_type=F32)

    @pl.when(j == pl.num_programs(1) - 1)
    def _():
        o_ref[...] = x_ref[...] + (1.0 + gt_ref[...]) * acc[...]


def _ffn_layer(x, mod, nrm, w_gu, w_down, tm, tf):
    bsz, seq, d = x.shape
    ff = w_down.shape[0]
    nt = seq // tm
    nj = ff // tf
    row = lambda k: pl.BlockSpec((None, 1, d), lambda i, j, k=k: (i // nt, 0, k))
    tile = pl.BlockSpec((None, tm, d), lambda i, j: (i // nt, i % nt, 0))
    return pl.pallas_call(
        _ffn_kernel,
        grid=(bsz * nt, nj),
        in_specs=[tile, row(3), row(4), row(5), pl.BlockSpec(nrm.shape, lambda i, j: (0, 0)),
                  pl.BlockSpec((d, tf), lambda i, j: (0, j)),
                  pl.BlockSpec((d, tf), lambda i, j: (0, j + nj)),
                  pl.BlockSpec((tf, d), lambda i, j: (j, 0))],
        out_specs=tile,
        out_shape=jax.ShapeDtypeStruct(x.shape, F32),
        scratch_shapes=[pltpu.VMEM((tm, d), BF16), pltpu.VMEM((tm, d), F32)],
        compiler_params=_cparams("parallel", "arbitrary"),
        name="ffn_layer",
    )(x, mod, mod, mod, nrm, w_gu, w_gu, w_down)


def _top2_gates(logits, n_experts):
    lane = lax.broadcasted_iota(jnp.int32, logits.shape, 1)
    logits = jnp.where(lane < n_experts, logits, -jnp.inf)
    m1 = jnp.max(logits, axis=1, keepdims=True)
    i1 = jnp.min(jnp.where(logits == m1, lane, LANES), axis=1, keepdims=True)
    rest = jnp.where(lane == i1, -jnp.inf, logits)
    m2 = jnp.max(rest, axis=1, keepdims=True)
    i2 = jnp.min(jnp.where(rest == m2, lane, LANES), axis=1, keepdims=True)
    w1 = 1.0 / (1.0 + jnp.exp(m2 - m1))
    return jnp.where(lane == i1, w1, 0.0) + jnp.where(lane == i2, 1.0 - w1, 0.0)


def _router_logits(h, rhi_ref, rlo_ref):
    h_hi = h.astype(BF16)
    h_lo = (h - h_hi.astype(F32)).astype(BF16)
    return (jnp.dot(h_hi, rhi_ref[...], preferred_element_type=F32)
            + jnp.dot(h_lo, rhi_ref[...], preferred_element_type=F32)
            + jnp.dot(h_hi, rlo_ref[...], preferred_element_type=F32))


def _moe_dense_kernel(n_experts, x_ref, sh_ref, sc_ref, gt_ref, nrm_ref, rhi_ref, rlo_ref,
                      wg_ref, wu_ref, wd_ref, o_ref, h_s, gates_s, acc):
    e = pl.program_id(1)
    j = pl.program_id(2)

    @pl.when((e == 0) & (j == 0))
    def _():
        h = _modulate(x_ref[...], nrm_ref[...], sh_ref[...], sc_ref[...])
        h_s[...] = h.astype(BF16)
        gates_s[...] = _top2_gates(_router_logits(h, rhi_ref, rlo_ref), n_experts)
        acc[...] = jnp.zeros_like(acc)

    h = h_s[...]
    gates = gates_s[...]
    lane = lax.broadcasted_iota(jnp.int32, gates.shape, 1)
    gcol = jnp.sum(jnp.where(lane == e, gates, 0.0), axis=1, keepdims=True)
    g = jnp.dot(h, wg_ref[...], preferred_element_type=F32)
    u = jnp.dot(h, wu_ref[...], preferred_element_type=F32)
    act = (_silu(g) * u * gcol).astype(BF16)
    acc[...] += jnp.dot(act, wd_ref[...], preferred_element_type=F32)

    @pl.when((e == pl.num_programs(1) - 1) & (j == pl.num_programs(2) - 1))
    def _():
        o_ref[...] = x_ref[...] + (1.0 + gt_ref[...]) * acc[...]


def _moe_dense_layer(x, mod, nrm, r_hi, r_lo, w_gu, w_down, tm, tf):
    bsz, seq, d = x.shape
    n_experts, ffe = w_down.shape[0], w_down.shape[1]
    nt = seq // tm
    nj = ffe // tf
    row = lambda k: pl.BlockSpec((None, 1, d), lambda i, e, j, k=k: (i // nt, 0, k))
    tile = pl.BlockSpec((None, tm, d), lambda i, e, j: (i // nt, i % nt, 0))
    const = lambda a: pl.BlockSpec(a.shape, lambda i, e, j, n=a.ndim: (0,) * n)
    return pl.pallas_call(
        functools.partial(_moe_dense_kernel, n_experts),
        grid=(bsz * nt, n_experts, nj),
        in_specs=[tile, row(3), row(4), row(5), const(nrm), const(r_hi), const(r_lo),
                  pl.BlockSpec((None, d, tf), lambda i, e, j: (e, 0, j)),
                  pl.BlockSpec((None, d, tf), lambda i, e, j: (e, 0, j + nj)),
                  pl.BlockSpec((None, tf, d), lambda i, e, j: (e, j, 0))],
        out_specs=tile,
        out_shape=jax.ShapeDtypeStruct(x.shape, F32),
        scratch_shapes=[pltpu.VMEM((tm, d), BF16), pltpu.VMEM((tm, LANES), F32), pltpu.VMEM((tm, d), F32)],
        compiler_params=_cparams("parallel", "arbitrary", "arbitrary"),
        name="moe_layer",
    )(x, mod, mod, mod, nrm, r_hi, r_lo, w_gu, w_gu, w_down)


def _pick(n, pref):
    return pref if n % pref == 0 else n


def kernel(x, c, positions, ada_w, ada_b, mix_norm, ffn_norm, a_w_in, a_conv_w, a_conv_b, a_gx_w, a_gx_b, a_ga_w, a_ga_b, a_lambda, a_w_out, kv_norm, kv_ada_w, kv_ada_b, w_kv, k_norm, b_w_q, b_q_norm, b_sinks, b_w_o, f_w_gu, f_w_down, m_router, m_w_gu, m_w_down):
    bsz, seq, d = x.shape
    depth = ada_w.shape[0]
    n_a = a_w_in.shape[0]
    n_experts = m_router.shape[-1]
    ts = _pick(seq, 512)
    tm = _pick(seq, 1024)
    tf = 512

    mods = _cond_linear(c, ada_w, ada_b).reshape(depth, bsz, 1, 6 * d)
    kvmod = _cond_linear(c, kv_ada_w[None], kv_ada_b[None]).reshape(bsz, 1, 2 * d)
    cos_t, sin_t = _rope_tables(positions)

    row = lambda v: v.reshape(1, -1)
    col = lambda v, n: jnp.broadcast_to(v.astype(F32)[:, None], (v.shape[0], n))
    k_arr = vt_arr = None
    for l in range(depth):
        mod = mods[l]
        if l < n_a:
            wg = jnp.concatenate([a_ga_w[l], a_gx_w[l]], axis=-1).astype(BF16)
            x = _rglru_layer(x, mod, row(mix_norm[l]), a_w_in[l].astype(BF16), a_conv_w[l], row(a_conv_b[l]),
                             wg, row(a_ga_b[l]), row(a_gx_b[l]), row(a_lambda[l]), a_w_out[l].astype(BF16), ts)
        else:
            if l == n_a:
                k_arr, vt_arr = _shared_kv(x, kvmod, row(kv_norm), w_kv.T.astype(BF16), col(k_norm, ts),
                                           cos_t, sin_t, ts)
            bi = l - n_a
            sink_row = jnp.repeat(b_sinks[bi].astype(F32), BLOCK).reshape(1, -1)
            x = _attn_layer(x, mod, row(mix_norm[l]), b_w_q[bi].T.astype(BF16), col(b_q_norm[bi], ts),
                            cos_t, sin_t, k_arr, vt_arr, sink_row, b_w_o[bi].astype(BF16), ts)
        if l % 2 == 0:
            x = _ffn_layer(x, mod, row(ffn_norm[l]), f_w_gu[l // 2].astype(BF16),
                           f_w_down[l // 2].astype(BF16), tm, tf)
        else:
            r_pad = jnp.pad(m_router[l // 2], ((0, 0), (0, LANES - n_experts)))
            r_hi = r_pad.astype(BF16)
            r_lo = (r_pad - r_hi.astype(F32)).astype(BF16)
            x = _moe_dense_layer(x, mod, row(ffn_norm[l]), r_hi, r_lo, m_w_gu[l // 2].astype(BF16),
                                 m_w_down[l // 2].astype(BF16), tm, tf)
    return x
```

```python
import functools

import jax
import jax.numpy as jnp
from jax import lax
from jax.experimental import pallas as pl
from jax.experimental.pallas import tpu as pltpu

F32 = jnp.float32
BF16 = jnp.bfloat16

EPS = 1e-6
HEAD_DIM = 64
N_KV_HEADS = 4
GROUP = 4
BLOCK = 128
ROPE_THETA = 10000.0
LRU_C = 8.0
RNN_BLOCK_W = 128
TOP_K = 2
SUBLANES = 8
LANES = 128
ROW_ALIGN = 16
RUN_CHUNK = 128
TOK_UNROLL = 8
VMEM_LIMIT = 56 * 1024 * 1024


def _cparams(*sem):
    return pltpu.CompilerParams(dimension_semantics=sem, vmem_limit_bytes=VMEM_LIMIT)


def _modulate(x, g, shift, scale):
    ms = jnp.mean(x * x, axis=-1, keepdims=True)
    return (x * lax.rsqrt(ms + EPS)) * g * (1.0 + scale) + shift


def _silu(v):
    return v * jax.nn.sigmoid(v)


def _cond_linear_kernel(c_ref, w_ref, b_ref, o_ref):
    ca = _silu(c_ref[...]).astype(BF16)
    o_ref[...] = jnp.dot(ca, w_ref[...].astype(BF16), preferred_element_type=F32) + b_ref[...]


def _cond_linear(c, w, b, tn=1024):
    nl, d, n = w.shape
    bsz = c.shape[0]
    return pl.pallas_call(
        _cond_linear_kernel,
        grid=(nl, n // tn),
        in_specs=[pl.BlockSpec((bsz, d), lambda l, j: (0, 0)),
                  pl.BlockSpec((None, d, tn), lambda l, j: (l, 0, j)),
                  pl.BlockSpec((None, 1, tn), lambda l, j: (l, 0, j))],
        out_specs=pl.BlockSpec((None, bsz, tn), lambda l, j: (l, 0, j)),
        out_shape=jax.ShapeDtypeStruct((nl, bsz, n), F32),
        compiler_params=_cparams("parallel", "parallel"),
        name="cond_linear",
    )(c, w, b.reshape(nl, 1, n))


def _rope_kernel(pos_ref, inv_ref, cos_ref, sin_ref):
    ang = inv_ref[...] * pos_ref[...].astype(F32)
    cos_ref[...] = jnp.cos(ang)
    sin_ref[...] = jnp.sin(ang)


def _rope_tables(positions):
    bsz, seq = positions.shape
    half = HEAD_DIM // 2
    inv = ROPE_THETA ** (-jnp.arange(0, HEAD_DIM, 2, dtype=F32) / HEAD_DIM)
    out = jax.ShapeDtypeStruct((bsz, half, seq), F32)
    return pl.pallas_call(
        _rope_kernel,
        grid=(bsz,),
        in_specs=[pl.BlockSpec((None, 1, seq), lambda b: (b, 0, 0)),
                  pl.BlockSpec((half, 1), lambda b: (0, 0))],
        out_specs=[pl.BlockSpec((None, half, seq), lambda b: (b, 0, 0))] * 2,
        out_shape=[out, out],
        compiler_params=_cparams("parallel"),
        name="rope_tables",
    )(positions.reshape(bsz, 1, seq), inv.reshape(half, 1))


def _norm_rotate_heads(t, n_heads, norm_col, cos_t, sin_t, post_scale):
    half = HEAD_DIM // 2
    outs = []
    for hh in range(n_heads):
        th = t[hh * HEAD_DIM:(hh + 1) * HEAD_DIM, :]
        ms = jnp.mean(th * th, axis=0, keepdims=True)
        th = th * lax.rsqrt(ms + EPS) * norm_col
        x1, x2 = th[:half, :], th[half:, :]
        r = jnp.concatenate([x1 * cos_t - x2 * sin_t, x2 * cos_t + x1 * sin_t], axis=0)
        outs.append(r * post_scale if post_scale != 1.0 else r)
    return outs


def _rglru_kernel(x_ref, sh_ref, sc_ref, gt_ref, nrm_ref, win_ref, cw_ref, cb_ref, wg_ref,
                  gab_ref, gxb_ref, lam_ref, wout_ref, o_ref, xbuf, a_s, b_s, hst):
    ts, d = x_ref.shape
    tail = SUBLANES

    @pl.when(pl.program_id(1) == 0)
    def _():
        xbuf[0:tail, :] = jnp.zeros((tail, d), F32)
        hst[...] = jnp.zeros_like(hst)

    x = x_ref[...]
    h = _modulate(x, nrm_ref[...], sh_ref[...], sc_ref[...]).astype(BF16)
    xy = jnp.dot(h, win_ref[...], preferred_element_type=F32)
    xb, yb = xy[:, :d], xy[:, d:]

    xbuf[tail:tail + ts, :] = xb
    cw = cw_ref[...]
    nk = cw.shape[0]
    xc = cb_ref[...] + xb * cw[nk - 1:nk, :]
    for j in range(1, nk):
        xc = xc + xbuf[tail - j:tail - j + ts, :] * cw[nk - 1 - j:nk - j, :]
    xbuf[0:tail, :] = xb[ts - tail:ts, :]

    xcb = xc.astype(BF16)
    nb = d // RNN_BLOCK_W
    zs = [jnp.dot(xcb[:, n * RNN_BLOCK_W:(n + 1) * RNN_BLOCK_W], wg_ref[n],
                  preferred_element_type=F32) for n in range(nb)]
    r_gate = jax.nn.sigmoid(jnp.concatenate([z[:, :RNN_BLOCK_W] for z in zs], axis=1) + gab_ref[...])
    i_gate = jax.nn.sigmoid(jnp.concatenate([z[:, RNN_BLOCK_W:] for z in zs], axis=1) + gxb_ref[...])
    log_a = (-LRU_C) * r_gate * jax.nn.softplus(-lam_ref[...])
    a = jnp.exp(log_a)
    a_s[...] = a
    b_s[...] = jnp.sqrt(1.0 - a * a) * (i_gate * xc)

    rowi = lax.broadcasted_iota(jnp.int32, (SUBLANES, d), 0)

    def group(ci, hprev):
        r0 = pl.multiple_of(ci * SUBLANES, SUBLANES)
        a8 = a_s[pl.ds(r0, SUBLANES), :]
        b8 = b_s[pl.ds(r0, SUBLANES), :]
        for dist in (1, 2, 4):
            keep = rowi >= dist
            a_sh = pltpu.roll(a8, dist, 0)
            b_sh = pltpu.roll(b8, dist, 0)
            b8 = jnp.where(keep, b8 + a8 * b_sh, b8)
            a8 = jnp.where(keep, a8 * a_sh, a8)
        hs8 = b8 + a8 * hprev
        b_s[pl.ds(r0, SUBLANES), :] = hs8
        return jnp.broadcast_to(hs8[SUBLANES - 1:SUBLANES, :], (SUBLANES, d))

    hst[...] = lax.fori_loop(0, ts // SUBLANES, group, hst[...])

    o = (b_s[...] * jax.nn.gelu(yb, approximate=True)).astype(BF16)
    y = jnp.dot(o, wout_ref[...], preferred_element_type=F32)
    o_ref[...] = x + (1.0 + gt_ref[...]) * y


def _rglru_layer(x, mod, nrm, w_in, conv_w, conv_b, wg, ga_b, gx_b, lam, w_out, ts):
    bsz, seq, d = x.shape
    row = lambda k: pl.BlockSpec((None, 1, d), lambda b, s, k=k: (b, 0, k))
    full = lambda a: pl.BlockSpec(a.shape, lambda b, s, n=a.ndim: (0,) * n)
    tile = pl.BlockSpec((None, ts, d), lambda b, s: (b, s, 0))
    return pl.pallas_call(
        _rglru_kernel,
        grid=(bsz, seq // ts),
        in_specs=[tile, row(0), row(1), row(2), full(nrm), full(w_in), full(conv_w), full(conv_b),
                  full(wg), full(ga_b), full(gx_b), full(lam), full(w_out)],
        out_specs=tile,
        out_shape=jax.ShapeDtypeStruct(x.shape, F32),
        scratch_shapes=[pltpu.VMEM((ts + SUBLANES, d), F32), pltpu.VMEM((ts, d), F32),
                        pltpu.VMEM((ts, d), F32), pltpu.VMEM((SUBLANES, d), F32)],
        compiler_params=_cparams("parallel", "arbitrary"),
        name="rglru_layer",
    )(x, mod, mod, mod, nrm, w_in, conv_w, conv_b, wg, ga_b, gx_b, lam, w_out)


def _kv_kernel(x_ref, sh_ref, sc_ref, nrm_ref, wkvt_ref, kn_ref, cos_ref, sin_ref, k_ref, vt_ref):
    h = _modulate(x_ref[...], nrm_ref[...], sh_ref[...], sc_ref[...]).astype(BF16)
    kvt = lax.dot_general(wkvt_ref[...], h, (((1,), (1,)), ((), ())), preferred_element_type=F32)
    nk = N_KV_HEADS * HEAD_DIM
    heads = _norm_rotate_heads(kvt[:nk, :], N_KV_HEADS, kn_ref[...], cos_ref[...], sin_ref[...], 1.0)
    k_ref[...] = jnp.concatenate(heads, axis=0).T.astype(BF16)
    vt_ref[...] = kvt[nk:, :].astype(BF16)


def _shared_kv(x, kvmod, nrm, w_kv_t, kn_col, cos_t, sin_t, tm):
    bsz, seq, d = x.shape
    nk = N_KV_HEADS * HEAD_DIM
    half = HEAD_DIM // 2
    row = lambda k: pl.BlockSpec((None, 1, d), lambda b, s, k=k: (b, 0, k))
    full = lambda a: pl.BlockSpec(a.shape, lambda b, s, n=a.ndim: (0,) * n)
    rope = pl.BlockSpec((None, half, tm), lambda b, s: (b, 0, s))
    return pl.pallas_call(
        _kv_kernel,
        grid=(bsz, seq // tm),
        in_specs=[pl.BlockSpec((None, tm, d), lambda b, s: (b, s, 0)), row(0), row(1), full(nrm),
                  full(w_kv_t), full(kn_col), rope, rope],
        out_specs=[pl.BlockSpec((None, tm, nk), lambda b, s: (b, s, 0)),
                   pl.BlockSpec((None, nk, tm), lambda b, s: (b, 0, s))],
        out_shape=[jax.ShapeDtypeStruct((bsz, seq, nk), BF16), jax.ShapeDtypeStruct((bsz, nk, seq), BF16)],
        compiler_params=_cparams("parallel", "parallel"),
        name="shared_kv",
    )(x, kvmod, kvmod, nrm, w_kv_t, kn_col, cos_t, sin_t)


def _attn_kernel(x_ref, sh_ref, sc_ref, gt_ref, nrm_ref, wqt_ref, qn_ref, cos_ref, sin_ref,
                 kc_ref, kp_ref, vc_ref, vp_ref, sink_ref, wo_ref, o_ref, ot_s):
    tq, d = x_ref.shape
    nqb = tq // BLOCK
    n_heads = d // HEAD_DIM
    gw = GROUP * BLOCK
    x = x_ref[...]
    h = _modulate(x, nrm_ref[...], sh_ref[...], sc_ref[...]).astype(BF16)
    qt = lax.dot_general(wqt_ref[...], h, (((1,), (1,)), ((), ())), preferred_element_type=F32)
    heads = _norm_rotate_heads(qt, n_heads, qn_ref[...], cos_ref[...], sin_ref[...], HEAD_DIM ** -0.5)
    heads = [q.astype(BF16) for q in heads]

    kj = lax.broadcasted_iota(jnp.int32, (2 * BLOCK, gw), 0)
    qi = lax.broadcasted_iota(jnp.int32, (2 * BLOCK, gw), 1) % BLOCK
    band = (kj > qi) & (kj <= qi + BLOCK)
    first_tile = pl.program_id(1) == 0
    zero_q = jnp.zeros((HEAD_DIM, gw), BF16)

    for jb in range(nqb):
        lo, hi = jb * BLOCK, (jb + 1) * BLOCK
        if jb == 0:
            k_prev, v_prev = kp_ref[...], vp_ref[...]
            mask = band & (jnp.logical_not(first_tile) | (kj >= BLOCK))
        else:
            k_prev, v_prev = kc_ref[lo - BLOCK:lo, :], vc_ref[:, lo - BLOCK:lo]
            mask = band
        k2 = jnp.concatenate([k_prev, kc_ref[lo:hi, :]], axis=0)
        v2t = jnp.concatenate([v_prev, vc_ref[:, lo:hi]], axis=1)
        for kh in range(N_KV_HEADS):
            qblk = jnp.concatenate([heads[kh * GROUP + g][:, lo:hi] for g in range(GROUP)], axis=1)
            qbd = jnp.concatenate([zero_q] * kh + [qblk] + [zero_q] * (N_KV_HEADS - 1 - kh), axis=0)
            s = jnp.dot(k2, qbd, preferred_element_type=F32)
            s = jnp.where(mask, s, -jnp.inf)
            sk = sink_ref[:, kh * gw:(kh + 1) * gw]
            m = jnp.maximum(jnp.max(s, axis=0, keepdims=True), sk)
            e = jnp.exp(s - m)
            den = jnp.sum(e, axis=0, keepdims=True) + jnp.exp(sk - m)
            p = (e * (1.0 / den)).astype(BF16)
            ot = jnp.dot(v2t[kh * HEAD_DIM:(kh + 1) * HEAD_DIM, :], p, preferred_element_type=F32)
            for g in range(GROUP):
                r0 = (kh * GROUP + g) * HEAD_DIM
                ot_s[r0:r0 + HEAD_DIM, lo:hi] = ot[:, g * BLOCK:(g + 1) * BLOCK]

    o = ot_s[...].T.astype(BF16)
    y = jnp.dot(o, wo_ref[...], preferred_element_type=F32)
    o_ref[...] = x + (1.0 + gt_ref[...]) * y


def _attn_layer(x, mod, nrm, w_q_t, qn_col, cos_t, sin_t, k, v_t, sink_row, w_o, tq):
    bsz, seq, d = x.shape
    nk = N_KV_HEADS * HEAD_DIM
    half = HEAD_DIM // 2
    nqb = tq // BLOCK
    row = lambda kk: pl.BlockSpec((None, 1, d), lambda b, s, kk=kk: (b, 0, kk))
    full = lambda a: pl.BlockSpec(a.shape, lambda b, s, n=a.ndim: (0,) * n)
    tile = pl.BlockSpec((None, tq, d), lambda b, s: (b, s, 0))
    rope = pl.BlockSpec((None, half, tq), lambda b, s: (b, 0, s))
    prev = lambda s: jnp.maximum(s * nqb - 1, 0)
    return pl.pallas_call(
        _attn_kernel,
        grid=(bsz, seq // tq),
        in_specs=[tile, row(0), row(1), row(2), full(nrm), full(w_q_t), full(qn_col), rope, rope,
                  pl.BlockSpec((None, tq, nk), lambda b, s: (b, s, 0)),
                  pl.BlockSpec((None, BLOCK, nk), lambda b, s: (b, prev(s), 0)),
                  pl.BlockSpec((None, nk, tq), lambda b, s: (b, 0, s)),
                  pl.BlockSpec((None, nk, BLOCK), lambda b, s: (b, 0, prev(s))),
                  full(sink_row), full(w_o)],
        out_specs=tile,
        out_shape=jax.ShapeDtypeStruct(x.shape, F32),
        scratch_shapes=[pltpu.VMEM((d, tq), F32)],
        compiler_params=_cparams("parallel", "parallel"),
        name="attn_layer",
    )(x, mod, mod, mod, nrm, w_q_t, qn_col, cos_t, sin_t, k, k, v_t, v_t, sink_row, w_o)


def _ffn_kernel(x_ref, sh_ref, sc_ref, gt_ref, nrm_ref, wg_ref, wu_ref, wd_ref, o_ref, h_s, acc):
    j = pl.program_id(1)

    @pl.when(j == 0)
    def _():
        h_s[...] = _modulate(x_ref[...], nrm_ref[...], sh_ref[...], sc_ref[...]).astype(BF16)
        acc[...] = jnp.zeros_like(acc)

    h = h_s[...]
    g = jnp.dot(h, wg_ref[...], preferred_element_type=F32)
    u = jnp.dot(h, wu_ref[...], preferred_element_type=F32)
    acc[...] += jnp.dot((_silu(g) * u).astype(BF16), wd_ref[...], preferred_element_type=F32)

    @pl.when(j == pl.num_programs(1) - 1)
    def _():
        o_ref[...] = x_ref[...] + (1.0 + gt_ref[...]) * acc[...]


def _ffn_layer(x, mod, nrm, w_gu, w_down, tm, tf):
    bsz, seq, d = x.shape
    ff = w_down.shape[0]
    nt = seq // tm
    nj = ff // tf
    row = lambda k: pl.BlockSpec((None, 1, d), lambda i, j, k=k: (i // nt, 0, k))
    tile = pl.BlockSpec((None, tm, d), lambda i, j: (i // nt, i % nt, 0))
    return pl.pallas_call(
        _ffn_kernel,
        grid=(bsz * nt, nj),
        in_specs=[tile, row(3), row(4), row(5), pl.BlockSpec(nrm.shape, lambda i, j: (0, 0)),
                  pl.BlockSpec((d, tf), lambda i, j: (0, j)),
                  pl.BlockSpec((d, tf), lambda i, j: (0, j + nj)),
                  pl.BlockSpec((tf, d), lambda i, j: (j, 0))],
        out_specs=tile,
        out_shape=jax.ShapeDtypeStruct(x.shape, F32),
        scratch_shapes=[pltpu.VMEM((tm, d), BF16), pltpu.VMEM((tm, d), F32)],
        compiler_params=_cparams("parallel", "arbitrary"),
        name="ffn_layer",
    )(x, mod, mod, mod, nrm, w_gu, w_gu, w_down)


def _top2(logits, n_experts):
    lane = lax.broadcasted_iota(jnp.int32, logits.shape, 1)
    logits = jnp.where(lane < n_experts, logits, -jnp.inf)
    m1 = jnp.max(logits, axis=1, keepdims=True)
    i1 = jnp.min(jnp.where(logits == m1, lane, LANES), axis=1, keepdims=True)
    rest = jnp.where(lane == i1, -jnp.inf, logits)
    m2 = jnp.max(rest, axis=1, keepdims=True)
    i2 = jnp.min(jnp.where(rest == m2, lane, LANES), axis=1, keepdims=True)
    w1 = 1.0 / (1.0 + jnp.exp(m2 - m1))
    return lane, i1, i2, w1


def _router_logits(h, rhi_ref, rlo_ref):
    h_hi = h.astype(BF16)
    h_lo = (h - h_hi.astype(F32)).astype(BF16)
    return (jnp.dot(h_hi, rhi_ref[...], preferred_element_type=F32)
            + jnp.dot(h_lo, rhi_ref[...], preferred_element_type=F32)
            + jnp.dot(h_hi, rlo_ref[...], preferred_element_type=F32))


_M_E1, _M_E2, _M_R1, _M_R2, _M_W1, _M_W2 = range(6)


def _router_kernel(n_experts, x_ref, sh_ref, sc_ref, nrm_ref, rhi_ref, rlo_ref,
                   h_ref, meta_ref, cnt_ref, carry):
    tr = x_ref.shape[0]

    @pl.when(pl.program_id(1) == 0)
    def _():
        carry[...] = jnp.zeros_like(carry)

    h = _modulate(x_ref[...], nrm_ref[...], sh_ref[...], sc_ref[...])
    h_ref[...] = h.astype(BF16)
    lane, i1, i2, w1 = _top2(_router_logits(h, rhi_ref, rlo_ref), n_experts)
    sel = (lane == i1) | (lane == i2)
    r = lax.broadcasted_iota(jnp.int32, (tr, tr), 0)
    cc = lax.broadcasted_iota(jnp.int32, (tr, tr), 1)
    ltri = jnp.where(cc < r, 1.0, 0.0).astype(BF16)
    onehot = jnp.where(sel, 1.0, 0.0)
    rank_all = jnp.dot(ltri, onehot.astype(BF16), preferred_element_type=F32) + carry[...]
    rank1 = jnp.sum(jnp.where(lane == i1, rank_all, 0.0), axis=1, keepdims=True)
    rank2 = jnp.sum(jnp.where(lane == i2, rank_all, 0.0), axis=1, keepdims=True)
    meta = jnp.zeros(lane.shape, F32)
    for idx, val in ((_M_E1, i1.astype(F32)), (_M_E2, i2.astype(F32)), (_M_R1, rank1), (_M_R2, rank2),
                     (_M_W1, w1), (_M_W2, 1.0 - w1)):
        meta = jnp.where(lane == idx, val, meta)
    meta_ref[...] = meta
    carry[...] += jnp.sum(onehot, axis=0, keepdims=True)
    cnt_ref[...] = carry[...]


def _moe_router(x, mod, nrm, r_hi, r_lo, n_experts, nb, tr):
    bsz, seq, d = x.shape
    t = bsz * seq
    nblk, nst = t // nb, nb // tr
    per_seq = seq // nb
    x2 = x.reshape(t, d)
    row = lambda k: pl.BlockSpec((None, 1, d), lambda b, s, k=k: (b // per_seq, 0, k))
    const = lambda a: pl.BlockSpec(a.shape, lambda b, s, n=a.ndim: (0,) * n)
    tile = lambda w: pl.BlockSpec((tr, w), lambda b, s: (b * nst + s, 0))
    return pl.pallas_call(
        functools.partial(_router_kernel, n_experts),
        grid=(nblk, nst),
        in_specs=[tile(d), row(3), row(4), const(nrm), const(r_hi), const(r_lo)],
        out_specs=[tile(d), tile(LANES), pl.BlockSpec((None, 1, LANES), lambda b, s: (b, 0, 0))],
        out_shape=[jax.ShapeDtypeStruct((t, d), BF16), jax.ShapeDtypeStruct((t, LANES), F32),
                   jax.ShapeDtypeStruct((nblk, 1, LANES), F32)],
        scratch_shapes=[pltpu.VMEM((1, LANES), F32)],
        compiler_params=_cparams("parallel", "arbitrary"),
        name="moe_router",
    )(x2, mod, mod, nrm, r_hi, r_lo)


def _row_slab(ref, r):
    return ref.at[pl.ds(pl.multiple_of(r * SUBLANES, SUBLANES), SUBLANES), :]


def _run_copies(n_experts, b, cnt_ref, goff_ref, hbm, stage, sem, to_hbm, wait):
    def copy(loc_row, glob_row, rows):
        loc = stage.at[pl.ds(pl.multiple_of(loc_row, ROW_ALIGN), rows), :]
        glob = hbm.at[pl.ds(pl.multiple_of(glob_row, ROW_ALIGN), rows), :]
        dma = pltpu.make_async_copy(loc, glob, sem) if to_hbm else pltpu.make_async_copy(glob, loc, sem)
        if wait:
            dma.wait()
        else:
            dma.start()

    base = jnp.int32(0)
    for e in range(n_experts):
        n = cnt_ref[b * n_experts + e]
        g = goff_ref[b * n_experts + e]
        nfull = n // RUN_CHUNK
        rem = n - nfull * RUN_CHUNK

        def full(ci, carry, base=base, g=g):
            copy(base + ci * RUN_CHUNK, g + ci * RUN_CHUNK, RUN_CHUNK)
            return carry

        lax.fori_loop(0, nfull, full, 0)
        piece = RUN_CHUNK // 2
        while piece >= ROW_ALIGN:
            off = nfull * RUN_CHUNK + rem - rem % (2 * piece)

            @pl.when(rem % (2 * piece) >= piece)
            def _(base=base, g=g, off=off, piece=piece):
                copy(base + off, g + off, piece)

            piece //= 2
        base = base + n


def _dispatch_kernel(n_experts, cnt_ref, goff_ref, h_ref, dest_ref, xs_in, xs_hbm,
                     hslab, xslab, stage, sem):
    del xs_in
    b, st = pl.program_id(0), pl.program_id(1)
    tr, d = h_ref.shape
    nsl = d // LANES
    cap = stage.shape[0]

    @pl.when(st == 0)
    def _():
        xslab[...] = jnp.zeros_like(xslab)

    hf = h_ref[...].astype(F32)
    for j in range(nsl):
        hslab[pl.ds(j, tr, stride=nsl), :] = hf[:, j * LANES:(j + 1) * LANES]

    def scatter(i, carry):
        for u in range(TOK_UNROLL):
            t = i * TOK_UNROLL + u
            slab = _row_slab(hslab, t)[...]
            _row_slab(xslab, dest_ref[0, 2 * t])[...] = slab
            _row_slab(xslab, dest_ref[0, 2 * t + 1])[...] = slab
        return carry

    lax.fori_loop(0, tr // TOK_UNROLL, scatter, 0)

    @pl.when(st == pl.num_programs(1) - 1)
    def _():
        for c in range(cap // RUN_CHUNK):
            r0 = c * RUN_CHUNK
            tile = jnp.concatenate([xslab[pl.ds(r0 * nsl + j, RUN_CHUNK, stride=nsl), :] for j in range(nsl)],
                                   axis=1)
            stage[r0:r0 + RUN_CHUNK, :] = tile.astype(BF16)
        _run_copies(n_experts, b, cnt_ref, goff_ref, xs_hbm, stage, sem, True, False)
        _run_copies(n_experts, b, cnt_ref, goff_ref, xs_hbm, stage, sem, True, True)


def _moe_dispatch(h, dest, cnt_pad, run_goff, n_rows, n_experts, nb, tr, cap):
    t, d = h.shape
    nblk, nst = t // nb, nb // tr
    xs0 = jnp.zeros((n_rows, d), BF16)
    grid_spec = pltpu.PrefetchScalarGridSpec(
        num_scalar_prefetch=2,
        grid=(nblk, nst),
        in_specs=[pl.BlockSpec((tr, d), lambda b, s, *_: (b * nst + s, 0)),
                  pl.BlockSpec((None, None, 1, 2 * tr), lambda b, s, *_: (b, s, 0, 0), memory_space=pltpu.SMEM),
                  pl.BlockSpec(memory_space=pl.ANY)],
        out_specs=pl.BlockSpec(memory_space=pl.ANY),
        scratch_shapes=[pltpu.VMEM((tr * (d // LANES), LANES), F32),
                        pltpu.VMEM((cap * (d // LANES), LANES), F32),
                        pltpu.VMEM((cap, d), BF16),
                        pltpu.SemaphoreType.DMA(())],
    )
    return pl.pallas_call(
        functools.partial(_dispatch_kernel, n_experts),
        grid_spec=grid_spec,
        out_shape=jax.ShapeDtypeStruct((n_rows, d), BF16),
        input_output_aliases={4: 0},
        compiler_params=_cparams("arbitrary", "arbitrary"),
        name="moe_dispatch",
    )(cnt_pad, run_goff, h, dest, xs0)


def _gmm_kernel(te_ref, xrow_ref, valid_ref, xs_ref, wg_ref, wu_ref, wd_ref, ys_ref, acc):
    del te_ref, xrow_ref
    i, j = pl.program_id(0), pl.program_id(1)

    @pl.when(valid_ref[i] != 0)
    def _():
        @pl.when(j == 0)
        def _():
            acc[...] = jnp.zeros_like(acc)

        h = xs_ref[...]
        g = jnp.dot(h, wg_ref[...], preferred_element_type=F32)
        u = jnp.dot(h, wu_ref[...], preferred_element_type=F32)
        acc[...] += jnp.dot((_silu(g) * u).astype(BF16), wd_ref[...], preferred_element_type=F32)

        @pl.when(j == pl.num_programs(1) - 1)
        def _():
            ys_ref[...] = acc[...].astype(BF16)

    @pl.when((valid_ref[i] == 0) & (j == 0))
    def _():
        ys_ref[...] = jnp.zeros_like(ys_ref)


def _moe_experts(xs, tile_expert, tile_row, tile_valid, w_gu, w_down, tm, tf):
    n_rows, d = xs.shape
    ffe = w_down.shape[1]
    nj = ffe // tf
    grid_spec = pltpu.PrefetchScalarGridSpec(
        num_scalar_prefetch=3,
        grid=(n_rows // tm, nj),
        in_specs=[pl.BlockSpec((tm, d), lambda i, j, te, xr, va: (xr[i], 0)),
                  pl.BlockSpec((None, d, tf), lambda i, j, te, xr, va: (te[i], 0, j)),
                  pl.BlockSpec((None, d, tf), lambda i, j, te, xr, va: (te[i], 0, j + nj)),
                  pl.BlockSpec((None, tf, d), lambda i, j, te, xr, va: (te[i], j, 0))],
        out_specs=pl.BlockSpec((tm, d), lambda i, j, te, xr, va: (i, 0)),
        scratch_shapes=[pltpu.VMEM((tm, d), F32)],
    )
    return pl.pallas_call(
        _gmm_kernel,
        grid_spec=grid_spec,
        out_shape=jax.ShapeDtypeStruct((n_rows, d), BF16),
        compiler_params=_cparams("arbitrary", "arbitrary"),
        name="moe_experts",
    )(tile_expert, tile_row, tile_valid, xs, w_gu, w_gu, w_down)


def _combine_kernel(n_experts, cnt_ref, goff_ref, x_ref, gt_ref, dest_ref, w_ref, ys_hbm, o_ref,
                    yslab, stage, mslab, sem):
    b, st = pl.program_id(0), pl.program_id(1)
    tr, d = x_ref.shape
    nsl = d // LANES
    cap = stage.shape[0]

    @pl.when(st == 0)
    def _():
        stage[...] = jnp.zeros_like(stage)
        _run_copies(n_experts, b, cnt_ref, goff_ref, ys_hbm, stage, sem, False, False)
        _run_copies(n_experts, b, cnt_ref, goff_ref, ys_hbm, stage, sem, False, True)
        for c in range(cap // RUN_CHUNK):
            r0 = c * RUN_CHUNK
            tile = stage[r0:r0 + RUN_CHUNK, :].astype(F32)
            for j in range(nsl):
                yslab[pl.ds(r0 * nsl + j, RUN_CHUNK, stride=nsl), :] = tile[:, j * LANES:(j + 1) * LANES]

    def gather(i, carry):
        for u in range(TOK_UNROLL):
            t = i * TOK_UNROLL + u
            y1 = _row_slab(yslab, dest_ref[0, 2 * t])[...]
            y2 = _row_slab(yslab, dest_ref[0, 2 * t + 1])[...]
            _row_slab(mslab, t)[...] = w_ref[0, 2 * t] * y1 + w_ref[0, 2 * t + 1] * y2
        return carry

    lax.fori_loop(0, tr // TOK_UNROLL, gather, 0)
    moe = jnp.concatenate([mslab[pl.ds(j, tr, stride=nsl), :] for j in range(nsl)], axis=1)
    o_ref[...] = x_ref[...] + (1.0 + gt_ref[...]) * moe


def _moe_combine(x, mod, dest, wts, ys, cnt_pad, run_goff, n_experts, nb, tr, cap):
    bsz, seq, d = x.shape
    t = bsz * seq
    nblk, nst = t // nb, nb // tr
    per_seq = seq // nb
    smem = lambda: pl.BlockSpec((None, None, 1, 2 * tr), lambda b, s, *_: (b, s, 0, 0), memory_space=pltpu.SMEM)
    tile = pl.BlockSpec((tr, d), lambda b, s, *_: (b * nst + s, 0))
    grid_spec = pltpu.PrefetchScalarGridSpec(
        num_scalar_prefetch=2,
        grid=(nblk, nst),
        in_specs=[tile, pl.BlockSpec((None, 1, d), lambda b, s, *_: (b // per_seq, 0, 5)), smem(), smem(),
                  pl.BlockSpec(memory_space=pl.ANY)],
        out_specs=tile,
        scratch_shapes=[pltpu.VMEM((cap * (d // LANES), LANES), F32),
                        pltpu.VMEM((cap, d), BF16),
                        pltpu.VMEM((tr * (d // LANES), LANES), F32),
                        pltpu.SemaphoreType.DMA(())],
    )
    out = pl.pallas_call(
        functools.partial(_combine_kernel, n_experts),
        grid_spec=grid_spec,
        out_shape=jax.ShapeDtypeStruct((t, d), F32),
        compiler_params=_cparams("arbitrary", "arbitrary"),
        name="moe_combine",
    )(cnt_pad, run_goff, x.reshape(t, d), mod, dest, wts, ys)
    return out.reshape(bsz, seq, d)


def _round_up(v, m):
    return (v + m - 1) // m * m


def _moe_layer(x, mod, nrm, router, w_gu, w_down, nb, tr, tm, tf):
    bsz, seq, d = x.shape
    t = bsz * seq
    n_experts = router.shape[-1]
    nblk, nst = t // nb, nb // tr
    r_pad = jnp.pad(router, ((0, 0), (0, LANES - n_experts)))
    r_hi = r_pad.astype(BF16)
    r_lo = (r_pad - r_hi.astype(F32)).astype(BF16)
    h, meta, cnt = _moe_router(x, mod, nrm, r_hi, r_lo, n_experts, nb, tr)

    cnt = cnt[:, 0, :n_experts].astype(jnp.int32)
    cnt_pad = _round_up(cnt, ROW_ALIGN)
    local_off = jnp.cumsum(cnt_pad, axis=1) - cnt_pad
    region = _round_up(jnp.sum(cnt_pad, axis=0), tm)
    region_end = jnp.cumsum(region)
    run_goff = (region_end - region)[None, :] + jnp.cumsum(cnt_pad, axis=0) - cnt_pad
    n_rows = _round_up(TOP_K * t + nblk * n_experts * ROW_ALIGN + n_experts * tm, tm)
    cap = _round_up(TOP_K * nb + n_experts * ROW_ALIGN, RUN_CHUNK)
    tile_start = jnp.arange(n_rows // tm, dtype=jnp.int32) * tm
    tile_expert = jnp.minimum(jnp.sum(tile_start[:, None] >= region_end[None, :], axis=1), n_experts - 1)
    tile_valid = (tile_start < region_end[-1]).astype(jnp.int32)
    tile_row = jnp.minimum(tile_start // tm, region_end[-1] // tm - 1)
    experts = meta[:, _M_E1:_M_E2 + 1].astype(jnp.int32)
    ranks = meta[:, _M_R1:_M_R2 + 1].astype(jnp.int32)
    onehot = experts[:, :, None] == jnp.arange(n_experts, dtype=jnp.int32)
    tok_off = jnp.repeat(local_off, nb, axis=0)
    dest = (jnp.sum(jnp.where(onehot, tok_off[:, None, :], 0), axis=-1) + ranks).reshape(nblk, nst, 1, 2 * tr)
    wts = meta[:, _M_W1:_M_W2 + 1].reshape(nblk, nst, 1, 2 * tr)
    cnt_flat, goff_flat = cnt_pad.reshape(-1), run_goff.reshape(-1).astype(jnp.int32)

    xs = _moe_dispatch(h, dest, cnt_flat, goff_flat, n_rows, n_experts, nb, tr, cap)
    ys = _moe_experts(xs, tile_expert.astype(jnp.int32), tile_row.astype(jnp.int32), tile_valid, w_gu, w_down, tm, tf)
    return _moe_combine(x, mod, dest, wts, ys, cnt_flat, goff_flat, n_experts, nb, tr, cap)


def _pick(n, pref):
    return pref if n % pref == 0 else n


def kernel(x, c, positions, ada_w, ada_b, mix_norm, ffn_norm, a_w_in, a_conv_w, a_conv_b, a_gx_w, a_gx_b, a_ga_w, a_ga_b, a_lambda, a_w_out, kv_norm, kv_ada_w, kv_ada_b, w_kv, k_norm, b_w_q, b_q_norm, b_sinks, b_w_o, f_w_gu, f_w_down, m_router, m_w_gu, m_w_down):
    bsz, seq, d = x.shape
    depth = ada_w.shape[0]
    n_a = a_w_in.shape[0]
    ts = _pick(seq, 512)
    tm = _pick(seq, 1024)
    tf = 512
    nb = _pick(seq, 2048)
    tr = _pick(nb, 512)

    mods = _cond_linear(c, ada_w, ada_b).reshape(depth, bsz, 1, 6 * d)
    kvmod = _cond_linear(c, kv_ada_w[None], kv_ada_b[None]).reshape(bsz, 1, 2 * d)
    cos_t, sin_t = _rope_tables(positions)

    row = lambda v: v.reshape(1, -1)
    col = lambda v, n: jnp.broadcast_to(v.astype(F32)[:, None], (v.shape[0], n))
    k_arr = vt_arr = None
    for l in range(depth):
        mod = mods[l]
        if l < n_a:
            wg = jnp.concatenate([a_ga_w[l], a_gx_w[l]], axis=-1).astype(BF16)
            x = _rglru_layer(x, mod, row(mix_norm[l]), a_w_in[l].astype(BF16), a_conv_w[l], row(a_conv_b[l]),
                             wg, row(a_ga_b[l]), row(a_gx_b[l]), row(a_lambda[l]), a_w_out[l].astype(BF16), ts)
        else:
            if l == n_a:
                k_arr, vt_arr = _shared_kv(x, kvmod, row(kv_norm), w_kv.T.astype(BF16), col(k_norm, ts),
                                           cos_t, sin_t, ts)
            bi = l - n_a
            sink_row = jnp.repeat(b_sinks[bi].astype(F32), BLOCK).reshape(1, -1)
            x = _attn_layer(x, mod, row(mix_norm[l]), b_w_q[bi].T.astype(BF16), col(b_q_norm[bi], ts),
                            cos_t, sin_t, k_arr, vt_arr, sink_row, b_w_o[bi].astype(BF16), ts)
        if l % 2 == 0:
            x = _ffn_layer(x, mod, row(ffn_norm[l]), f_w_gu[l // 2].astype(BF16),
                           f_w_down[l // 2].astype(BF16), tm, tf)
        else:
            x = _moe_layer(x, mod, row(ffn_norm[l]), m_router[l // 2], m_w_gu[l // 2].astype(BF16),
                           m_w_down[l // 2].astype(BF16), nb, tr, tm, tf)
    return x
```

```python
import functools

import jax
import jax.numpy as jnp
from jax import lax
from jax.experimental import pallas as pl
from jax.experimental.pallas import tpu as pltpu

F32 = jnp.float32
BF16 = jnp.bfloat16

EPS = 1e-6
HEAD_DIM = 64
N_KV_HEADS = 4
GROUP = 4
BLOCK = 128
ROPE_THETA = 10000.0
LRU_C = 8.0
LOG2E = 1.4426950408889634
RNN_BLOCK_W = 128
TOP_K = 2
SUBLANES = 8
LANES = 128
ROW_ALIGN = 16
RUN_CHUNK = 128
TOK_UNROLL = 8
RGLRU_PARTS = 2
VMEM_LIMIT = 56 * 1024 * 1024


def _cparams(*sem):
    return pltpu.CompilerParams(dimension_semantics=sem, vmem_limit_bytes=VMEM_LIMIT)


def _modulate(x, g, shift, scale):
    ms = jnp.mean(x * x, axis=-1, keepdims=True)
    return (x * lax.rsqrt(ms + EPS)) * (g * (1.0 + scale)) + shift


def _silu(v):
    return v * jax.nn.sigmoid(v)


def _cond_linear_kernel(c_ref, w_ref, b_ref, o_ref):
    ca = _silu(c_ref[...]).astype(BF16)
    o_ref[...] = jnp.dot(ca, w_ref[...].astype(BF16), preferred_element_type=F32) + b_ref[...]


def _cond_linear(c, w, b, tn=1024):
    nl, d, n = w.shape
    bsz = c.shape[0]
    return pl.pallas_call(
        _cond_linear_kernel,
        grid=(nl, n // tn),
        in_specs=[pl.BlockSpec((bsz, d), lambda l, j: (0, 0)),
                  pl.BlockSpec((None, d, tn), lambda l, j: (l, 0, j)),
                  pl.BlockSpec((None, 1, tn), lambda l, j: (l, 0, j))],
        out_specs=pl.BlockSpec((None, bsz, tn), lambda l, j: (l, 0, j)),
        out_shape=jax.ShapeDtypeStruct((nl, bsz, n), F32),
        compiler_params=_cparams("parallel", "parallel"),
        name="cond_linear",
    )(c, w, b.reshape(nl, 1, n))


def _rope_kernel(pos_ref, inv_ref, cos_ref, sin_ref):
    ang = inv_ref[...] * pos_ref[...].astype(F32)
    cos_ref[...] = jnp.cos(ang)
    sin_ref[...] = jnp.sin(ang)


def _rope_tables(positions):
    bsz, seq = positions.shape
    half = HEAD_DIM // 2
    inv = ROPE_THETA ** (-jnp.arange(0, HEAD_DIM, 2, dtype=F32) / HEAD_DIM)
    out = jax.ShapeDtypeStruct((bsz, half, seq), F32)
    return pl.pallas_call(
        _rope_kernel,
        grid=(bsz,),
        in_specs=[pl.BlockSpec((None, 1, seq), lambda b: (b, 0, 0)),
                  pl.BlockSpec((half, 1), lambda b: (0, 0))],
        out_specs=[pl.BlockSpec((None, half, seq), lambda b: (b, 0, 0))] * 2,
        out_shape=[out, out],
        compiler_params=_cparams("parallel"),
        name="rope_tables",
    )(positions.reshape(bsz, 1, seq), inv.reshape(half, 1))


def _norm_rotate_heads(t, n_heads, norm_col, cos_t, sin_t):
    half = HEAD_DIM // 2
    outs = []
    for hh in range(n_heads):
        th = t[hh * HEAD_DIM:(hh + 1) * HEAD_DIM, :]
        ms = jnp.mean(th * th, axis=0, keepdims=True)
        th = th * lax.rsqrt(ms + EPS) * norm_col
        x1, x2 = th[:half, :], th[half:, :]
        outs.append(jnp.concatenate([x1 * cos_t - x2 * sin_t, x2 * cos_t + x1 * sin_t], axis=0))
    return outs


def _to_slab(ref, row0, tile):
    rows, d = tile.shape
    nsl = d // LANES
    for j in range(nsl):
        ref[pl.ds(row0 + j, rows, stride=nsl), :] = tile[:, j * LANES:(j + 1) * LANES]


def _from_slab(ref, row0, rows, d):
    nsl = d // LANES
    return jnp.concatenate([ref[pl.ds(row0 + j, rows, stride=nsl), :] for j in range(nsl)], axis=1)


def _rglru_kernel(x_ref, sh_ref, sc_ref, gt_ref, nrm_ref, win_ref, cw_ref, cb_ref, wg_ref,
                  gab_ref, gxb_ref, lam_ref, wout_ref, o_ref, xsl, csl, asl, bsl, hst):
    ts, d = x_ref.shape
    nsl = d // LANES
    nk = cw_ref.shape[0]
    hist = (nk - 1) * nsl

    @pl.when(pl.program_id(1) == 0)
    def _():
        xsl[0:hist, :] = jnp.zeros((hist, LANES), F32)
        hst[...] = jnp.zeros_like(hst)

    parts = [(p * ts // RGLRU_PARTS, (p + 1) * ts // RGLRU_PARTS) for p in range(RGLRU_PARTS)]
    xs = [x_ref[r0:r1, :] for r0, r1 in parts]
    hs = [_modulate(xp, nrm_ref[...], sh_ref[...], sc_ref[...]).astype(BF16) for xp in xs]
    xys = [jnp.dot(hp, win_ref[...], preferred_element_type=F32) for hp in hs]

    cw = cw_ref[...]
    xcs = []
    for (r0, r1), xy in zip(parts, xys):
        rows = r1 - r0
        _to_slab(xsl, hist + r0 * nsl, xy[:, :d])
        xs3 = xsl[r0 * nsl:(r1 + nk - 1) * nsl, :].reshape(rows + nk - 1, nsl, LANES)
        xc3 = cb_ref[...][None] + cw[0][None] * xs3[0:rows]
        for k in range(1, nk):
            xc3 = xc3 + cw[k][None] * xs3[k:k + rows]
        csl[r0 * nsl:r1 * nsl, :] = xc3.reshape(rows * nsl, LANES)
        xcs.append(_from_slab(csl, r0 * nsl, rows, d))
    xsl[0:hist, :] = xsl[ts * nsl:ts * nsl + hist, :]

    nb = d // RNN_BLOCK_W
    zss = []
    for xc in xcs:
        xcb = xc.astype(BF16)
        zss.append([jnp.dot(xcb[:, n * RNN_BLOCK_W:(n + 1) * RNN_BLOCK_W], wg_ref[n],
                            preferred_element_type=F32) for n in range(nb)])
    sp = jax.nn.softplus(-lam_ref[...])
    for (r0, r1), xc, zs in zip(parts, xcs, zss):
        r_gate = jax.nn.sigmoid(jnp.concatenate([z[:, :RNN_BLOCK_W] for z in zs], axis=1) + gab_ref[...])
        i_gate = jax.nn.sigmoid(jnp.concatenate([z[:, RNN_BLOCK_W:] for z in zs], axis=1) + gxb_ref[...])
        a = jnp.exp((-LRU_C) * r_gate * sp)
        _to_slab(asl, r0 * nsl, a)
        _to_slab(bsl, r0 * nsl, jnp.sqrt(1.0 - a * a) * (i_gate * xc))

    hcur = hst[...]
    for (r0, r1), xp, xy in zip(parts, xs, xys):
        for t in range(r0, r1):
            hcur = asl[t * nsl:(t + 1) * nsl, :] * hcur + bsl[t * nsl:(t + 1) * nsl, :]
            bsl[t * nsl:(t + 1) * nsl, :] = hcur
        o = (_from_slab(bsl, r0 * nsl, r1 - r0, d) * jax.nn.gelu(xy[:, d:], approximate=True)).astype(BF16)
        y = jnp.dot(o, wout_ref[...], preferred_element_type=F32)
        o_ref[r0:r1, :] = xp + (1.0 + gt_ref[...]) * y
    hst[...] = hcur


def _rglru_layer(x, mod, nrm, w_in, conv_w, conv_b, wg, ga_b, gx_b, lam, w_out, ts):
    bsz, seq, d = x.shape
    nsl = d // LANES
    nk = conv_w.shape[0]
    conv_w = conv_w.reshape(nk, nsl, LANES)
    conv_b = conv_b.reshape(nsl, LANES)
    row = lambda k: pl.BlockSpec((None, 1, d), lambda b, s, k=k: (b, 0, k))
    full = lambda a: pl.BlockSpec(a.shape, lambda b, s, n=a.ndim: (0,) * n)
    tile = pl.BlockSpec((None, ts, d), lambda b, s: (b, s, 0))
    slab = pltpu.VMEM((ts * nsl, LANES), F32)
    return pl.pallas_call(
        _rglru_kernel,
        grid=(bsz, seq // ts),
        in_specs=[tile, row(0), row(1), row(2), full(nrm), full(w_in), full(conv_w), full(conv_b),
                  full(wg), full(ga_b), full(gx_b), full(lam), full(w_out)],
        out_specs=tile,
        out_shape=jax.ShapeDtypeStruct(x.shape, F32),
        scratch_shapes=[pltpu.VMEM(((ts + nk - 1) * nsl, LANES), F32), slab, slab, slab,
                        pltpu.VMEM((nsl, LANES), F32)],
        compiler_params=_cparams("parallel", "arbitrary"),
        name="rglru_layer",
    )(x, mod, mod, mod, nrm, w_in, conv_w, conv_b, wg, ga_b, gx_b, lam, w_out)


def _kv_kernel(x_ref, sh_ref, sc_ref, nrm_ref, wkvt_ref, kn_ref, cos_ref, sin_ref, k_ref, vt_ref):
    h = _modulate(x_ref[...], nrm_ref[...], sh_ref[...], sc_ref[...]).astype(BF16)
    kvt = lax.dot_general(wkvt_ref[...], h, (((1,), (1,)), ((), ())), preferred_element_type=F32)
    nk = N_KV_HEADS * HEAD_DIM
    heads = _norm_rotate_heads(kvt[:nk, :], N_KV_HEADS, kn_ref[...], cos_ref[...], sin_ref[...])
    k_ref[...] = jnp.concatenate(heads, axis=0).T.astype(BF16)
    vt_ref[...] = kvt[nk:, :].astype(BF16)


def _shared_kv(x, kvmod, nrm, w_kv_t, kn_col, cos_t, sin_t, tm):
    bsz, seq, d = x.shape
    nk = N_KV_HEADS * HEAD_DIM
    half = HEAD_DIM // 2
    row = lambda k: pl.BlockSpec((None, 1, d), lambda b, s, k=k: (b, 0, k))
    full = lambda a: pl.BlockSpec(a.shape, lambda b, s, n=a.ndim: (0,) * n)
    rope = pl.BlockSpec((None, half, tm), lambda b, s: (b, 0, s))
    return pl.pallas_call(
        _kv_kernel,
        grid=(bsz, seq // tm),
        in_specs=[pl.BlockSpec((None, tm, d), lambda b, s: (b, s, 0)), row(0), row(1), full(nrm),
                  full(w_kv_t), full(kn_col), rope, rope],
        out_specs=[pl.BlockSpec((None, tm, nk), lambda b, s: (b, s, 0)),
                   pl.BlockSpec((None, nk, tm), lambda b, s: (b, 0, s))],
        out_shape=[jax.ShapeDtypeStruct((bsz, seq, nk), BF16), jax.ShapeDtypeStruct((bsz, nk, seq), BF16)],
        compiler_params=_cparams("parallel", "parallel"),
        name="shared_kv",
    )(x, kvmod, kvmod, nrm, w_kv_t, kn_col, cos_t, sin_t)


def _attn_kernel(x_ref, sh_ref, sc_ref, gt_ref, nrm_ref, wqt_ref, qn_ref, cos_ref, sin_ref,
                 kc_ref, kp_ref, vc_ref, vp_ref, sink_ref, wo_ref, o_ref, ot_s, q_s):
    tq, d = x_ref.shape
    nqb = tq // BLOCK
    n_heads = d // HEAD_DIM
    gw = GROUP * BLOCK
    x = x_ref[...]
    h = _modulate(x, nrm_ref[...], sh_ref[...], sc_ref[...]).astype(BF16)
    qt = lax.dot_general(wqt_ref[...], h, (((1,), (1,)), ((), ())), preferred_element_type=F32)
    heads = _norm_rotate_heads(qt, n_heads, qn_ref[...], cos_ref[...], sin_ref[...])
    for hh in range(n_heads):
        q_s[hh * HEAD_DIM:(hh + 1) * HEAD_DIM, :] = heads[hh].astype(BF16)

    kj = lax.broadcasted_iota(jnp.int32, (2 * BLOCK, gw), 0)
    qi = lax.broadcasted_iota(jnp.int32, (2 * BLOCK, gw), 1) % BLOCK
    band = (kj > qi) & (kj <= qi + BLOCK)
    first_tile = pl.program_id(1) == 0
    zero_q = jnp.zeros((HEAD_DIM, gw), BF16)
    ones_v = jnp.ones((ROW_ALIGN, 2 * BLOCK), BF16)

    def scores(jb, kh):
        lo, hi = jb * BLOCK, (jb + 1) * BLOCK
        k_prev = kp_ref[...] if jb == 0 else kc_ref[lo - BLOCK:lo, :]
        k2 = jnp.concatenate([k_prev, kc_ref[lo:hi, :]], axis=0)
        qblk = jnp.concatenate([q_s[(kh * GROUP + g) * HEAD_DIM:(kh * GROUP + g + 1) * HEAD_DIM, lo:hi]
                                for g in range(GROUP)], axis=1)
        qbd = jnp.concatenate([zero_q] * kh + [qblk] + [zero_q] * (N_KV_HEADS - 1 - kh), axis=0)
        return jnp.dot(k2, qbd, preferred_element_type=F32)

    items = [(jb, kh) for jb in range(nqb) for kh in range(N_KV_HEADS)]
    s_next = scores(*items[0])
    for n, (jb, kh) in enumerate(items):
        lo, hi = jb * BLOCK, (jb + 1) * BLOCK
        s = s_next
        if n + 1 < len(items):
            s_next = scores(*items[n + 1])
        if jb == 0:
            v_prev = vp_ref[kh * HEAD_DIM:(kh + 1) * HEAD_DIM, :]
            mask = band & (jnp.logical_not(first_tile) | (kj >= BLOCK))
        else:
            v_prev = vc_ref[kh * HEAD_DIM:(kh + 1) * HEAD_DIM, lo - BLOCK:lo]
            mask = band
        s = jnp.where(mask, s, -jnp.inf)
        sk = sink_ref[:, kh * gw:(kh + 1) * gw]
        m = jnp.maximum(jnp.max(s, axis=0, keepdims=True), sk)
        e = jnp.exp2(s - m).astype(BF16)
        v_cur = vc_ref[kh * HEAD_DIM:(kh + 1) * HEAD_DIM, lo:hi]
        v_ext = jnp.concatenate([jnp.concatenate([v_prev, v_cur], axis=1), ones_v], axis=0)
        ot = jnp.dot(v_ext, e, preferred_element_type=F32)
        den = ot[HEAD_DIM:HEAD_DIM + 1, :] + jnp.exp2(sk - m)
        ot = ot[:HEAD_DIM, :] * (1.0 / den)
        for g in range(GROUP):
            r0 = (kh * GROUP + g) * HEAD_DIM
            ot_s[r0:r0 + HEAD_DIM, lo:hi] = ot[:, g * BLOCK:(g + 1) * BLOCK]

    o = ot_s[...].T.astype(BF16)
    y = jnp.dot(o, wo_ref[...], preferred_element_type=F32)
    o_ref[...] = x + (1.0 + gt_ref[...]) * y


def _attn_layer(x, mod, nrm, w_q_t, qn_col, cos_t, sin_t, k, v_t, sink_row, w_o, tq):
    bsz, seq, d = x.shape
    nk = N_KV_HEADS * HEAD_DIM
    half = HEAD_DIM // 2
    nqb = tq // BLOCK
    row = lambda kk: pl.BlockSpec((None, 1, d), lambda b, s, kk=kk: (b, 0, kk))
    full = lambda a: pl.BlockSpec(a.shape, lambda b, s, n=a.ndim: (0,) * n)
    tile = pl.BlockSpec((None, tq, d), lambda b, s: (b, s, 0))
    rope = pl.BlockSpec((None, half, tq), lambda b, s: (b, 0, s))
    prev = lambda s: jnp.maximum(s * nqb - 1, 0)
    return pl.pallas_call(
        _attn_kernel,
        grid=(bsz, seq // tq),
        in_specs=[tile, row(0), row(1), row(2), full(nrm), full(w_q_t), full(qn_col), rope, rope,
                  pl.BlockSpec((None, tq, nk), lambda b, s: (b, s, 0)),
                  pl.BlockSpec((None, BLOCK, nk), lambda b, s: (b, prev(s), 0)),
                  pl.BlockSpec((None, nk, tq), lambda b, s: (b, 0, s)),
                  pl.BlockSpec((None, nk, BLOCK), lambda b, s: (b, 0, prev(s))),
                  full(sink_row), full(w_o)],
        out_specs=tile,
        out_shape=jax.ShapeDtypeStruct(x.shape, F32),
        scratch_shapes=[pltpu.VMEM((d, tq), F32), pltpu.VMEM((d, tq), BF16)],
        compiler_params=_cparams("parallel", "parallel"),
        name="attn_layer",
    )(x, mod, mod, mod, nrm, w_q_t, qn_col, cos_t, sin_t, k, k, v_t, v_t, sink_row, w_o)


def _ffn_kernel(x_ref, sh_ref, sc_ref, gt_ref, nrm_ref, wg_ref, wu_ref, wd_ref, o_ref, h_s, acc):
    j = pl.program_id(1)

    @pl.when(j == 0)
    def _():
        h_s[...] = _modulate(x_ref[...], nrm_ref[...], sh_ref[...], sc_ref[...]).astype(BF16)
        acc[...] = jnp.zeros_like(acc)

    h = h_s[...]
    g = jnp.dot(h, wg_ref[...], preferred_element_type=F32)
    u = jnp.dot(h, wu_ref[...], preferred_element_type=F32)
    acc[...] += jnp.dot((_silu(g) * u).astype(BF16), wd_ref[...], preferred_element_type=F32)

    @pl.when(j == pl.num_programs(1) - 1)
    def _():
        o_ref[...] = x_ref[...] + (1.0 + gt_ref[...]) * acc[...]


def _ffn_layer(x, mod, nrm, w_gu, w_down, tm, tf):
    bsz, seq, d = x.shape
    ff = w_down.shape[0]
    nt = seq // tm
    nj = ff // tf
    row = lambda k: pl.BlockSpec((None, 1, d), lambda i, j, k=k: (i // nt, 0, k))
    tile = pl.BlockSpec((None, tm, d), lambda i, j: (i // nt, i % nt, 0))
    return pl.pallas_call(
        _ffn_kernel,
        grid=(bsz * nt, nj),
        in_specs=[tile, row(3), row(4), row(5), pl.BlockSpec(nrm.shape, lambda i, j: (0, 0)),
                  pl.BlockSpec((d, tf), lambda i, j: (0, j)),
                  pl.BlockSpec((d, tf), lambda i, j: (0, j + nj)),
                  pl.BlockSpec((tf, d), lambda i, j: (j, 0))],
        out_specs=tile,
        out_shape=jax.ShapeDtypeStruct(x.shape, F32),
        scratch_shapes=[pltpu.VMEM((tm, d), BF16), pltpu.VMEM((tm, d), F32)],
        compiler_params=_cparams("parallel", "arbitrary"),
        name="ffn_layer",
    )(x, mod, mod, mod, nrm, w_gu, w_gu, w_down)


def _top2(logits, n_experts):
    lane = lax.broadcasted_iota(jnp.int32, logits.shape, 1)
    logits = jnp.where(lane < n_experts, logits, -jnp.inf)
    m1 = jnp.max(logits, axis=1, keepdims=True)
    i1 = jnp.min(jnp.where(logits == m1, lane, LANES), axis=1, keepdims=True)
    rest = jnp.where(lane == i1, -jnp.inf, logits)
    m2 = jnp.max(rest, axis=1, keepdims=True)
    i2 = jnp.min(jnp.where(rest == m2, lane, LANES), axis=1, keepdims=True)
    w1 = 1.0 / (1.0 + jnp.exp(m2 - m1))
    return lane, i1, i2, w1


def _router_logits(h, rhi_ref, rlo_ref):
    h_hi = h.astype(BF16)
    h_lo = (h - h_hi.astype(F32)).astype(BF16)
    return (jnp.dot(h_hi, rhi_ref[...], preferred_element_type=F32)
            + jnp.dot(h_lo, rhi_ref[...], preferred_element_type=F32)
            + jnp.dot(h_hi, rlo_ref[...], preferred_element_type=F32))


_M_E1, _M_E2, _M_R1, _M_R2, _M_W1, _M_W2 = range(6)


def _router_kernel(n_experts, x_ref, sh_ref, sc_ref, nrm_ref, rhi_ref, rlo_ref,
                   h_ref, meta_ref, cnt_ref, carry):
    tr = x_ref.shape[0]

    @pl.when(pl.program_id(1) == 0)
    def _():
        carry[...] = jnp.zeros_like(carry)

    h = _modulate(x_ref[...], nrm_ref[...], sh_ref[...], sc_ref[...])
    h_ref[...] = h.astype(BF16)
    lane, i1, i2, w1 = _top2(_router_logits(h, rhi_ref, rlo_ref), n_experts)
    sel = (lane == i1) | (lane == i2)
    r = lax.broadcasted_iota(jnp.int32, (tr, tr), 0)
    cc = lax.broadcasted_iota(jnp.int32, (tr, tr), 1)
    ltri = jnp.where(cc < r, 1.0, 0.0).astype(BF16)
    onehot = jnp.where(sel, 1.0, 0.0)
    rank_all = jnp.dot(ltri, onehot.astype(BF16), preferred_element_type=F32) + carry[...]
    rank1 = jnp.sum(jnp.where(lane == i1, rank_all, 0.0), axis=1, keepdims=True)
    rank2 = jnp.sum(jnp.where(lane == i2, rank_all, 0.0), axis=1, keepdims=True)
    meta = jnp.zeros(lane.shape, F32)
    for idx, val in ((_M_E1, i1.astype(F32)), (_M_E2, i2.astype(F32)), (_M_R1, rank1), (_M_R2, rank2),
                     (_M_W1, w1), (_M_W2, 1.0 - w1)):
        meta = jnp.where(lane == idx, val, meta)
    meta_ref[...] = meta
    carry[...] += jnp.sum(onehot, axis=0, keepdims=True)
    cnt_ref[...] = carry[...]


def _moe_router(x, mod, nrm, r_hi, r_lo, n_experts, nb, tr):
    bsz, seq, d = x.shape
    t = bsz * seq
    nblk, nst = t // nb, nb // tr
    per_seq = seq // nb
    x2 = x.reshape(t, d)
    row = lambda k: pl.BlockSpec((None, 1, d), lambda b, s, k=k: (b // per_seq, 0, k))
    const = lambda a: pl.BlockSpec(a.shape, lambda b, s, n=a.ndim: (0,) * n)
    tile = lambda w: pl.BlockSpec((tr, w), lambda b, s: (b * nst + s, 0))
    return pl.pallas_call(
        functools.partial(_router_kernel, n_experts),
        grid=(nblk, nst),
        in_specs=[tile(d), row(3), row(4), const(nrm), const(r_hi), const(r_lo)],
        out_specs=[tile(d), tile(LANES), pl.BlockSpec((None, 1, LANES), lambda b, s: (b, 0, 0))],
        out_shape=[jax.ShapeDtypeStruct((t, d), BF16), jax.ShapeDtypeStruct((t, LANES), F32),
                   jax.ShapeDtypeStruct((nblk, 1, LANES), F32)],
        scratch_shapes=[pltpu.VMEM((1, LANES), F32)],
        compiler_params=_cparams("parallel", "arbitrary"),
        name="moe_router",
    )(x2, mod, mod, nrm, r_hi, r_lo)


def _row_slab(ref, r):
    return ref.at[pl.ds(pl.multiple_of(r * SUBLANES, SUBLANES), SUBLANES), :]


def _run_copies(n_experts, b, cnt_ref, goff_ref, hbm, stage, sem, to_hbm, wait):
    def copy(loc_row, glob_row, rows):
        loc = stage.at[pl.ds(pl.multiple_of(loc_row, ROW_ALIGN), rows), :]
        glob = hbm.at[pl.ds(pl.multiple_of(glob_row, ROW_ALIGN), rows), :]
        dma = pltpu.make_async_copy(loc, glob, sem) if to_hbm else pltpu.make_async_copy(glob, loc, sem)
        if wait:
            dma.wait()
        else:
            dma.start()

    base = jnp.int32(0)
    for e in range(n_experts):
        n = cnt_ref[b * n_experts + e]
        g = goff_ref[b * n_experts + e]
        nfull = n // RUN_CHUNK
        rem = n - nfull * RUN_CHUNK

        def full(ci, carry, base=base, g=g):
            copy(base + ci * RUN_CHUNK, g + ci * RUN_CHUNK, RUN_CHUNK)
            return carry

        lax.fori_loop(0, nfull, full, 0)
        piece = RUN_CHUNK // 2
        while piece >= ROW_ALIGN:
            off = nfull * RUN_CHUNK + rem - rem % (2 * piece)

            @pl.when(rem % (2 * piece) >= piece)
            def _(base=base, g=g, off=off, piece=piece):
                copy(base + off, g + off, piece)

            piece //= 2
        base = base + n


def _dispatch_kernel(n_experts, cnt_ref, goff_ref, h_ref, dest_ref, xs_in, xs_hbm,
                     hslab, xslab, stage, sem):
    del xs_in
    b, st = pl.program_id(0), pl.program_id(1)
    tr, d = h_ref.shape
    nsl = d // LANES
    cap = stage.shape[0]

    @pl.when((b == 0) & (st == 0))
    def _():
        xslab[...] = jnp.zeros_like(xslab)

    hf = h_ref[...].astype(F32)
    for j in range(nsl):
        hslab[pl.ds(j, tr, stride=nsl), :] = hf[:, j * LANES:(j + 1) * LANES]

    def scatter(i, carry):
        for u in range(TOK_UNROLL):
            t = i * TOK_UNROLL + u
            slab = _row_slab(hslab, t)[...]
            _row_slab(xslab, dest_ref[0, 2 * t])[...] = slab
            _row_slab(xslab, dest_ref[0, 2 * t + 1])[...] = slab
        return carry

    lax.fori_loop(0, tr // TOK_UNROLL, scatter, 0)

    @pl.when(st == pl.num_programs(1) - 1)
    def _():
        for c in range(cap // RUN_CHUNK):
            r0 = c * RUN_CHUNK
            tile = jnp.concatenate([xslab[pl.ds(r0 * nsl + j, RUN_CHUNK, stride=nsl), :] for j in range(nsl)],
                                   axis=1)
            stage[r0:r0 + RUN_CHUNK, :] = tile.astype(BF16)
        _run_copies(n_experts, b, cnt_ref, goff_ref, xs_hbm, stage, sem, True, False)
        _run_copies(n_experts, b, cnt_ref, goff_ref, xs_hbm, stage, sem, True, True)


def _moe_dispatch(h, dest, cnt_pad, run_goff, n_rows, n_experts, nb, tr, cap):
    t, d = h.shape
    nblk, nst = t // nb, nb // tr
    xs0 = jnp.zeros((n_rows, d), BF16)
    grid_spec = pltpu.PrefetchScalarGridSpec(
        num_scalar_prefetch=2,
        grid=(nblk, nst),
        in_specs=[pl.BlockSpec((tr, d), lambda b, s, *_: (b * nst + s, 0)),
                  pl.BlockSpec((None, None, 1, 2 * tr), lambda b, s, *_: (b, s, 0, 0), memory_space=pltpu.SMEM),
                  pl.BlockSpec(memory_space=pl.ANY)],
        out_specs=pl.BlockSpec(memory_space=pl.ANY),
        scratch_shapes=[pltpu.VMEM((tr * (d // LANES), LANES), F32),
                        pltpu.VMEM((cap * (d // LANES), LANES), F32),
                        pltpu.VMEM((cap, d), BF16),
                        pltpu.SemaphoreType.DMA(())],
    )
    return pl.pallas_call(
        functools.partial(_dispatch_kernel, n_experts),
        grid_spec=grid_spec,
        out_shape=jax.ShapeDtypeStruct((n_rows, d), BF16),
        input_output_aliases={4: 0},
        compiler_params=_cparams("arbitrary", "arbitrary"),
        name="moe_dispatch",
    )(cnt_pad, run_goff, h, dest, xs0)


def _gmm_kernel(te_ref, xrow_ref, valid_ref, xs_ref, wg_ref, wu_ref, wd_ref, ys_ref, acc):
    del te_ref, xrow_ref
    i, j = pl.program_id(0), pl.program_id(1)

    @pl.when(valid_ref[i] != 0)
    def _():
        @pl.when(j == 0)
        def _():
            acc[...] = jnp.zeros_like(acc)

        h = xs_ref[...]
        g = jnp.dot(h, wg_ref[...], preferred_element_type=F32)
        u = jnp.dot(h, wu_ref[...], preferred_element_type=F32)
        acc[...] += jnp.dot((_silu(g) * u).astype(BF16), wd_ref[...], preferred_element_type=F32)

        @pl.when(j == pl.num_programs(1) - 1)
        def _():
            ys_ref[...] = acc[...].astype(BF16)

    @pl.when((valid_ref[i] == 0) & (j == 0))
    def _():
        ys_ref[...] = jnp.zeros_like(ys_ref)


def _moe_experts(xs, tile_expert, tile_row, tile_valid, w_gu, w_down, tm, tf):
    n_rows, d = xs.shape
    ffe = w_down.shape[1]
    nj = ffe // tf
    grid_spec = pltpu.PrefetchScalarGridSpec(
        num_scalar_prefetch=3,
        grid=(n_rows // tm, nj),
        in_specs=[pl.BlockSpec((tm, d), lambda i, j, te, xr, va: (xr[i], 0)),
                  pl.BlockSpec((None, d, tf), lambda i, j, te, xr, va: (te[i], 0, j)),
                  pl.BlockSpec((None, d, tf), lambda i, j, te, xr, va: (te[i], 0, j + nj)),
                  pl.BlockSpec((None, tf, d), lambda i, j, te, xr, va: (te[i], j, 0))],
        out_specs=pl.BlockSpec((tm, d), lambda i, j, te, xr, va: (i, 0)),
        scratch_shapes=[pltpu.VMEM((tm, d), F32)],
    )
    return pl.pallas_call(
        _gmm_kernel,
        grid_spec=grid_spec,
        out_shape=jax.ShapeDtypeStruct((n_rows, d), BF16),
        compiler_params=_cparams("arbitrary", "arbitrary"),
        name="moe_experts",
    )(tile_expert, tile_row, tile_valid, xs, w_gu, w_gu, w_down)


def _combine_kernel(n_experts, cnt_ref, goff_ref, x_ref, gt_ref, dest_ref, w_ref, ys_hbm, o_ref,
                    yslab, stage, mslab, sem):
    b, st = pl.program_id(0), pl.program_id(1)
    tr, d = x_ref.shape
    nsl = d // LANES
    cap = stage.shape[0]

    @pl.when((b == 0) & (st == 0))
    def _():
        stage[...] = jnp.zeros_like(stage)

    @pl.when(st == 0)
    def _():
        _run_copies(n_experts, b, cnt_ref, goff_ref, ys_hbm, stage, sem, False, False)
        _run_copies(n_experts, b, cnt_ref, goff_ref, ys_hbm, stage, sem, False, True)
        for c in range(cap // RUN_CHUNK):
            r0 = c * RUN_CHUNK
            tile = stage[r0:r0 + RUN_CHUNK, :].astype(F32)
            for j in range(nsl):
                yslab[pl.ds(r0 * nsl + j, RUN_CHUNK, stride=nsl), :] = tile[:, j * LANES:(j + 1) * LANES]

    def gather(i, carry):
        for u in range(TOK_UNROLL):
            t = i * TOK_UNROLL + u
            y1 = _row_slab(yslab, dest_ref[0, 2 * t])[...]
            y2 = _row_slab(yslab, dest_ref[0, 2 * t + 1])[...]
            _row_slab(mslab, t)[...] = w_ref[0, 2 * t] * y1 + w_ref[0, 2 * t + 1] * y2
        return carry

    lax.fori_loop(0, tr // TOK_UNROLL, gather, 0)
    moe = jnp.concatenate([mslab[pl.ds(j, tr, stride=nsl), :] for j in range(nsl)], axis=1)
    o_ref[...] = x_ref[...] + (1.0 + gt_ref[...]) * moe


def _moe_combine(x, mod, dest, wts, ys, cnt_pad, run_goff, n_experts, nb, tr, cap):
    bsz, seq, d = x.shape
    t = bsz * seq
    nblk, nst = t // nb, nb // tr
    per_seq = seq // nb
    smem = lambda: pl.BlockSpec((None, None, 1, 2 * tr), lambda b, s, *_: (b, s, 0, 0), memory_space=pltpu.SMEM)
    tile = pl.BlockSpec((tr, d), lambda b, s, *_: (b * nst + s, 0))
    grid_spec = pltpu.PrefetchScalarGridSpec(
        num_scalar_prefetch=2,
        grid=(nblk, nst),
        in_specs=[tile, pl.BlockSpec((None, 1, d), lambda b, s, *_: (b // per_seq, 0, 5)), smem(), smem(),
                  pl.BlockSpec(memory_space=pl.ANY)],
        out_specs=tile,
        scratch_shapes=[pltpu.VMEM((cap * (d // LANES), LANES), F32),
                        pltpu.VMEM((cap, d), BF16),
                        pltpu.VMEM((tr * (d // LANES), LANES), F32),
                        pltpu.SemaphoreType.DMA(())],
    )
    out = pl.pallas_call(
        functools.partial(_combine_kernel, n_experts),
        grid_spec=grid_spec,
        out_shape=jax.ShapeDtypeStruct((t, d), F32),
        compiler_params=_cparams("arbitrary", "arbitrary"),
        name="moe_combine",
    )(cnt_pad, run_goff, x.reshape(t, d), mod, dest, wts, ys)
    return out.reshape(bsz, seq, d)


def _round_up(v, m):
    return (v + m - 1) // m * m


def _moe_layer(x, mod, nrm, router, w_gu, w_down, nb, tr, tm, tf):
    bsz, seq, d = x.shape
    t = bsz * seq
    n_experts = router.shape[-1]
    nblk, nst = t // nb, nb // tr
    r_pad = jnp.pad(router, ((0, 0), (0, LANES - n_experts)))
    r_hi = r_pad.astype(BF16)
    r_lo = (r_pad - r_hi.astype(F32)).astype(BF16)
    h, meta, cnt = _moe_router(x, mod, nrm, r_hi, r_lo, n_experts, nb, tr)

    cnt = cnt[:, 0, :n_experts].astype(jnp.int32)
    cnt_pad = _round_up(cnt, ROW_ALIGN)
    local_off = jnp.cumsum(cnt_pad, axis=1) - cnt_pad
    region = _round_up(jnp.sum(cnt_pad, axis=0), tm)
    region_end = jnp.cumsum(region)
    run_goff = (region_end - region)[None, :] + jnp.cumsum(cnt_pad, axis=0) - cnt_pad
    n_rows = _round_up(TOP_K * t + nblk * n_experts * ROW_ALIGN + n_experts * tm, tm)
    cap = _round_up(TOP_K * nb + n_experts * ROW_ALIGN, RUN_CHUNK)
    tile_start = jnp.arange(n_rows // tm, dtype=jnp.int32) * tm
    tile_expert = jnp.minimum(jnp.sum(tile_start[:, None] >= region_end[None, :], axis=1), n_experts - 1)
    tile_valid = (tile_start < region_end[-1]).astype(jnp.int32)
    tile_row = jnp.minimum(tile_start // tm, region_end[-1] // tm - 1)
    experts = meta[:, _M_E1:_M_E2 + 1].astype(jnp.int32)
    ranks = meta[:, _M_R1:_M_R2 + 1].astype(jnp.int32)
    onehot = experts[:, :, None] == jnp.arange(n_experts, dtype=jnp.int32)
    tok_off = jnp.repeat(local_off, nb, axis=0)
    dest = (jnp.sum(jnp.where(onehot, tok_off[:, None, :], 0), axis=-1) + ranks).reshape(nblk, nst, 1, 2 * tr)
    wts = meta[:, _M_W1:_M_W2 + 1].reshape(nblk, nst, 1, 2 * tr)
    cnt_flat, goff_flat = cnt_pad.reshape(-1), run_goff.reshape(-1).astype(jnp.int32)

    xs = _moe_dispatch(h, dest, cnt_flat, goff_flat, n_rows, n_experts, nb, tr, cap)
    ys = _moe_experts(xs, tile_expert.astype(jnp.int32), tile_row.astype(jnp.int32), tile_valid, w_gu, w_down, tm, tf)
    return _moe_combine(x, mod, dest, wts, ys, cnt_flat, goff_flat, n_experts, nb, tr, cap)


def _pick(n, pref):
    return pref if n % pref == 0 else n


def kernel(x, c, positions, ada_w, ada_b, mix_norm, ffn_norm, a_w_in, a_conv_w, a_conv_b, a_gx_w, a_gx_b, a_ga_w, a_ga_b, a_lambda, a_w_out, kv_norm, kv_ada_w, kv_ada_b, w_kv, k_norm, b_w_q, b_q_norm, b_sinks, b_w_o, f_w_gu, f_w_down, m_router, m_w_gu, m_w_down):
    bsz, seq, d = x.shape
    depth = ada_w.shape[0]
    n_a = a_w_in.shape[0]
    ts = _pick(seq, 512)
    tm = _pick(seq, 1024)
    tf = 512
    nb = _pick(seq, 2048)
    tr = _pick(nb, 512)

    mods = _cond_linear(c, ada_w, ada_b).reshape(depth, bsz, 1, 6 * d)
    kvmod = _cond_linear(c, kv_ada_w[None], kv_ada_b[None]).reshape(bsz, 1, 2 * d)
    cos_t, sin_t = _rope_tables(positions)

    row = lambda v: v.reshape(1, -1)
    col = lambda v, n: jnp.broadcast_to(v.astype(F32)[:, None], (v.shape[0], n))
    k_arr = vt_arr = None
    for l in range(depth):
        mod = mods[l]
        if l < n_a:
            wg = jnp.concatenate([a_ga_w[l], a_gx_w[l]], axis=-1).astype(BF16)
            x = _rglru_layer(x, mod, row(mix_norm[l]), a_w_in[l].astype(BF16), a_conv_w[l], row(a_conv_b[l]),
                             wg, row(a_ga_b[l]), row(a_gx_b[l]), row(a_lambda[l]), a_w_out[l].astype(BF16), ts)
        else:
            if l == n_a:
                k_arr, vt_arr = _shared_kv(x, kvmod, row(kv_norm), w_kv.T.astype(BF16), col(k_norm, ts),
                                           cos_t, sin_t, ts)
            bi = l - n_a
            sink_row = jnp.repeat(b_sinks[bi].astype(F32) * LOG2E, BLOCK).reshape(1, -1)
            q_gain = b_q_norm[bi].astype(F32) * (HEAD_DIM ** -0.5 * LOG2E)
            x = _attn_layer(x, mod, row(mix_norm[l]), b_w_q[bi].T.astype(BF16), col(q_gain, ts),
                            cos_t, sin_t, k_arr, vt_arr, sink_row, b_w_o[bi].astype(BF16), ts)
        if l % 2 == 0:
            x = _ffn_layer(x, mod, row(ffn_norm[l]), f_w_gu[l // 2].astype(BF16),
                           f_w_down[l // 2].astype(BF16), tm, tf)
        else:
            x = _moe_layer(x, mod, row(ffn_norm[l]), m_router[l // 2], m_w_gu[l // 2].astype(BF16),
                           m_w_down[l // 2].astype(BF16), nb, tr, tm, tf)
    return x
```

```python
import functools

import jax
import jax.numpy as jnp
from jax import lax
from jax.experimental import pallas as pl
from jax.experimental.pallas import tpu as pltpu

F32 = jnp.float32
BF16 = jnp.bfloat16

EPS = 1e-6
HEAD_DIM = 64
N_KV_HEADS = 4
GROUP = 4
BLOCK = 128
ROPE_THETA = 10000.0
LRU_C = 8.0
LOG2E = 1.4426950408889634
RNN_BLOCK_W = 128
TOP_K = 2
SUBLANES = 8
LANES = 128
ROW_ALIGN = 16
RUN_CHUNK = 128
TOK_UNROLL = 8
RGLRU_PARTS = 2
VMEM_LIMIT = 56 * 1024 * 1024


def _cparams(*sem):
    return pltpu.CompilerParams(dimension_semantics=sem, vmem_limit_bytes=VMEM_LIMIT)


def _modulate(x, g, shift, scale):
    ms = jnp.mean(x * x, axis=-1, keepdims=True)
    return (x * lax.rsqrt(ms + EPS)) * (g * (1.0 + scale)) + shift


def _silu(v):
    return v * jax.nn.sigmoid(v)


def _cond_linear_kernel(c_ref, w_ref, b_ref, o_ref):
    ca = _silu(c_ref[...]).astype(BF16)
    o_ref[...] = jnp.dot(ca, w_ref[...].astype(BF16), preferred_element_type=F32) + b_ref[...]


def _cond_linear(c, w, b, tn=1024):
    nl, d, n = w.shape
    bsz = c.shape[0]
    return pl.pallas_call(
        _cond_linear_kernel,
        grid=(nl, n // tn),
        in_specs=[pl.BlockSpec((bsz, d), lambda l, j: (0, 0)),
                  pl.BlockSpec((None, d, tn), lambda l, j: (l, 0, j)),
                  pl.BlockSpec((None, 1, tn), lambda l, j: (l, 0, j))],
        out_specs=pl.BlockSpec((None, bsz, tn), lambda l, j: (l, 0, j)),
        out_shape=jax.ShapeDtypeStruct((nl, bsz, n), F32),
        compiler_params=_cparams("parallel", "parallel"),
        name="cond_linear",
    )(c, w, b.reshape(nl, 1, n))


def _rope_kernel(pos_ref, inv_ref, cos_ref, sin_ref):
    ang = inv_ref[...] * pos_ref[...].astype(F32)
    cos_ref[...] = jnp.cos(ang)
    sin_ref[...] = jnp.sin(ang)


def _rope_tables(positions):
    bsz, seq = positions.shape
    half = HEAD_DIM // 2
    inv = ROPE_THETA ** (-jnp.arange(0, HEAD_DIM, 2, dtype=F32) / HEAD_DIM)
    out = jax.ShapeDtypeStruct((bsz, half, seq), F32)
    return pl.pallas_call(
        _rope_kernel,
        grid=(bsz,),
        in_specs=[pl.BlockSpec((None, 1, seq), lambda b: (b, 0, 0)),
                  pl.BlockSpec((half, 1), lambda b: (0, 0))],
        out_specs=[pl.BlockSpec((None, half, seq), lambda b: (b, 0, 0))] * 2,
        out_shape=[out, out],
        compiler_params=_cparams("parallel"),
        name="rope_tables",
    )(positions.reshape(bsz, 1, seq), inv.reshape(half, 1))


def _norm_rotate_heads(t, n_heads, norm_col, cos_t, sin_t):
    half = HEAD_DIM // 2
    outs = []
    for hh in range(n_heads):
        th = t[hh * HEAD_DIM:(hh + 1) * HEAD_DIM, :]
        ms = jnp.mean(th * th, axis=0, keepdims=True)
        th = th * lax.rsqrt(ms + EPS) * norm_col
        x1, x2 = th[:half, :], th[half:, :]
        outs.append(jnp.concatenate([x1 * cos_t - x2 * sin_t, x2 * cos_t + x1 * sin_t], axis=0))
    return outs


def _to_slab(ref, row0, tile):
    rows, d = tile.shape
    nsl = d // LANES
    for j in range(nsl):
        ref[pl.ds(row0 + j, rows, stride=nsl), :] = tile[:, j * LANES:(j + 1) * LANES]


def _from_slab(ref, row0, rows, d):
    nsl = d // LANES
    return jnp.concatenate([ref[pl.ds(row0 + j, rows, stride=nsl), :] for j in range(nsl)], axis=1)


def _rglru_kernel(x_ref, sh_ref, sc_ref, gt_ref, nrm_ref, win_ref, cw_ref, cb_ref, wg_ref,
                  gab_ref, gxb_ref, lam_ref, wout_ref, o_ref, xsl, csl, asl, bsl, hst):
    ts, d = x_ref.shape
    nsl = d // LANES
    nk = cw_ref.shape[0]
    hist = (nk - 1) * nsl

    @pl.when(pl.program_id(1) == 0)
    def _():
        xsl[0:hist, :] = jnp.zeros((hist, LANES), F32)
        hst[...] = jnp.zeros_like(hst)

    parts = [(p * ts // RGLRU_PARTS, (p + 1) * ts // RGLRU_PARTS) for p in range(RGLRU_PARTS)]
    xs = [x_ref[r0:r1, :] for r0, r1 in parts]
    hs = [_modulate(xp, nrm_ref[...], sh_ref[...], sc_ref[...]).astype(BF16) for xp in xs]
    xys = [jnp.dot(hp, win_ref[...], preferred_element_type=F32) for hp in hs]

    cw = cw_ref[...]
    xcs = []
    for (r0, r1), xy in zip(parts, xys):
        rows = r1 - r0
        _to_slab(xsl, hist + r0 * nsl, xy[:, :d])
        xs3 = xsl[r0 * nsl:(r1 + nk - 1) * nsl, :].reshape(rows + nk - 1, nsl, LANES)
        xc3 = cb_ref[...][None] + cw[0][None] * xs3[0:rows]
        for k in range(1, nk):
            xc3 = xc3 + cw[k][None] * xs3[k:k + rows]
        csl[r0 * nsl:r1 * nsl, :] = xc3.reshape(rows * nsl, LANES)
        xcs.append(_from_slab(csl, r0 * nsl, rows, d))
    xsl[0:hist, :] = xsl[ts * nsl:ts * nsl + hist, :]

    nb = d // RNN_BLOCK_W
    zss = []
    for xc in xcs:
        xcb = xc.astype(BF16)
        zss.append([jnp.dot(xcb[:, n * RNN_BLOCK_W:(n + 1) * RNN_BLOCK_W], wg_ref[n],
                            preferred_element_type=F32) for n in range(nb)])
    sp = jax.nn.softplus(-lam_ref[...])
    for (r0, r1), xc, zs in zip(parts, xcs, zss):
        r_gate = jax.nn.sigmoid(jnp.concatenate([z[:, :RNN_BLOCK_W] for z in zs], axis=1) + gab_ref[...])
        i_gate = jax.nn.sigmoid(jnp.concatenate([z[:, RNN_BLOCK_W:] for z in zs], axis=1) + gxb_ref[...])
        a = jnp.exp((-LRU_C) * r_gate * sp)
        _to_slab(asl, r0 * nsl, a)
        _to_slab(bsl, r0 * nsl, jnp.sqrt(1.0 - a * a) * (i_gate * xc))

    hcur = hst[...]
    for (r0, r1), xp, xy in zip(parts, xs, xys):
        for t in range(r0, r1):
            hcur = asl[t * nsl:(t + 1) * nsl, :] * hcur + bsl[t * nsl:(t + 1) * nsl, :]
            bsl[t * nsl:(t + 1) * nsl, :] = hcur
        o = (_from_slab(bsl, r0 * nsl, r1 - r0, d) * jax.nn.gelu(xy[:, d:], approximate=True)).astype(BF16)
        y = jnp.dot(o, wout_ref[...], preferred_element_type=F32)
        o_ref[r0:r1, :] = xp + (1.0 + gt_ref[...]) * y
    hst[...] = hcur


def _rglru_layer(x, mod, nrm, w_in, conv_w, conv_b, wg, ga_b, gx_b, lam, w_out, ts):
    bsz, seq, d = x.shape
    nsl = d // LANES
    nk = conv_w.shape[0]
    conv_w = conv_w.reshape(nk, nsl, LANES)
    conv_b = conv_b.reshape(nsl, LANES)
    row = lambda k: pl.BlockSpec((None, 1, d), lambda b, s, k=k: (b, 0, k))
    full = lambda a: pl.BlockSpec(a.shape, lambda b, s, n=a.ndim: (0,) * n)
    tile = pl.BlockSpec((None, ts, d), lambda b, s: (b, s, 0))
    slab = pltpu.VMEM((ts * nsl, LANES), F32)
    return pl.pallas_call(
        _rglru_kernel,
        grid=(bsz, seq // ts),
        in_specs=[tile, row(0), row(1), row(2), full(nrm), full(w_in), full(conv_w), full(conv_b),
                  full(wg), full(ga_b), full(gx_b), full(lam), full(w_out)],
        out_specs=tile,
        out_shape=jax.ShapeDtypeStruct(x.shape, F32),
        scratch_shapes=[pltpu.VMEM(((ts + nk - 1) * nsl, LANES), F32), slab, slab, slab,
                        pltpu.VMEM((nsl, LANES), F32)],
        compiler_params=_cparams("parallel", "arbitrary"),
        name="rglru_layer",
    )(x, mod, mod, mod, nrm, w_in, conv_w, conv_b, wg, ga_b, gx_b, lam, w_out)


def _kv_kernel(x_ref, sh_ref, sc_ref, nrm_ref, wkvt_ref, kn_ref, cos_ref, sin_ref, k_ref, vt_ref):
    h = _modulate(x_ref[...], nrm_ref[...], sh_ref[...], sc_ref[...]).astype(BF16)
    kvt = lax.dot_general(wkvt_ref[...], h, (((1,), (1,)), ((), ())), preferred_element_type=F32)
    nk = N_KV_HEADS * HEAD_DIM
    heads = _norm_rotate_heads(kvt[:nk, :], N_KV_HEADS, kn_ref[...], cos_ref[...], sin_ref[...])
    k_ref[...] = jnp.concatenate(heads, axis=0).T.astype(BF16)
    vt_ref[...] = kvt[nk:, :].astype(BF16)


def _shared_kv(x, kvmod, nrm, w_kv_t, kn_col, cos_t, sin_t, tm):
    bsz, seq, d = x.shape
    nk = N_KV_HEADS * HEAD_DIM
    half = HEAD_DIM // 2
    row = lambda k: pl.BlockSpec((None, 1, d), lambda b, s, k=k: (b, 0, k))
    full = lambda a: pl.BlockSpec(a.shape, lambda b, s, n=a.ndim: (0,) * n)
    rope = pl.BlockSpec((None, half, tm), lambda b, s: (b, 0, s))
    return pl.pallas_call(
        _kv_kernel,
        grid=(bsz, seq // tm),
        in_specs=[pl.BlockSpec((None, tm, d), lambda b, s: (b, s, 0)), row(0), row(1), full(nrm),
                  full(w_kv_t), full(kn_col), rope, rope],
        out_specs=[pl.BlockSpec((None, tm, nk), lambda b, s: (b, s, 0)),
                   pl.BlockSpec((None, nk, tm), lambda b, s: (b, 0, s))],
        out_shape=[jax.ShapeDtypeStruct((bsz, seq, nk), BF16), jax.ShapeDtypeStruct((bsz, nk, seq), BF16)],
        compiler_params=_cparams("parallel", "parallel"),
        name="shared_kv",
    )(x, kvmod, kvmod, nrm, w_kv_t, kn_col, cos_t, sin_t)


def _attn_kernel(x_ref, sh_ref, sc_ref, gt_ref, nrm_ref, wqt_ref, qn_ref, cos_ref, sin_ref,
                 kc_ref, kp_ref, vc_ref, vp_ref, sink_ref, wo_ref, o_ref, ot_s, q_s):
    tq, d = x_ref.shape
    nqb = tq // BLOCK
    n_heads = d // HEAD_DIM
    gw = GROUP * BLOCK
    x = x_ref[...]
    h = _modulate(x, nrm_ref[...], sh_ref[...], sc_ref[...]).astype(BF16)
    qt = lax.dot_general(wqt_ref[...], h, (((1,), (1,)), ((), ())), preferred_element_type=F32)
    heads = _norm_rotate_heads(qt, n_heads, qn_ref[...], cos_ref[...], sin_ref[...])
    for hh in range(n_heads):
        q_s[hh * HEAD_DIM:(hh + 1) * HEAD_DIM, :] = heads[hh].astype(BF16)

    kj = lax.broadcasted_iota(jnp.int32, (2 * BLOCK, gw), 0)
    qi = lax.broadcasted_iota(jnp.int32, (2 * BLOCK, gw), 1) % BLOCK
    band = (kj > qi) & (kj <= qi + BLOCK)
    first_tile = pl.program_id(1) == 0
    zero_q = jnp.zeros((HEAD_DIM, gw), BF16)
    ones_v = jnp.ones((ROW_ALIGN, 2 * BLOCK), BF16)

    def scores(jb, kh):
        lo, hi = jb * BLOCK, (jb + 1) * BLOCK
        k_prev = kp_ref[...] if jb == 0 else kc_ref[lo - BLOCK:lo, :]
        k2 = jnp.concatenate([k_prev, kc_ref[lo:hi, :]], axis=0)
        qblk = jnp.concatenate([q_s[(kh * GROUP + g) * HEAD_DIM:(kh * GROUP + g + 1) * HEAD_DIM, lo:hi]
                                for g in range(GROUP)], axis=1)
        qbd = jnp.concatenate([zero_q] * kh + [qblk] + [zero_q] * (N_KV_HEADS - 1 - kh), axis=0)
        return jnp.dot(k2, qbd, preferred_element_type=F32)

    items = [(jb, kh) for jb in range(nqb) for kh in range(N_KV_HEADS)]
    s_next = scores(*items[0])
    for n, (jb, kh) in enumerate(items):
        lo, hi = jb * BLOCK, (jb + 1) * BLOCK
        s = s_next
        if n + 1 < len(items):
            s_next = scores(*items[n + 1])
        if jb == 0:
            v_prev = vp_ref[kh * HEAD_DIM:(kh + 1) * HEAD_DIM, :]
            mask = band & (jnp.logical_not(first_tile) | (kj >= BLOCK))
        else:
            v_prev = vc_ref[kh * HEAD_DIM:(kh + 1) * HEAD_DIM, lo - BLOCK:lo]
            mask = band
        s = jnp.where(mask, s, -jnp.inf)
        sk = sink_ref[:, kh * gw:(kh + 1) * gw]
        m = jnp.maximum(jnp.max(s, axis=0, keepdims=True), sk)
        e = jnp.exp2(s - m).astype(BF16)
        v_cur = vc_ref[kh * HEAD_DIM:(kh + 1) * HEAD_DIM, lo:hi]
        v_ext = jnp.concatenate([jnp.concatenate([v_prev, v_cur], axis=1), ones_v], axis=0)
        ot = jnp.dot(v_ext, e, preferred_element_type=F32)
        den = ot[HEAD_DIM:HEAD_DIM + 1, :] + jnp.exp2(sk - m)
        ot = ot[:HEAD_DIM, :] * (1.0 / den)
        for g in range(GROUP):
            r0 = (kh * GROUP + g) * HEAD_DIM
            ot_s[r0:r0 + HEAD_DIM, lo:hi] = ot[:, g * BLOCK:(g + 1) * BLOCK]

    o = ot_s[...].T.astype(BF16)
    y = jnp.dot(o, wo_ref[...], preferred_element_type=F32)
    o_ref[...] = x + (1.0 + gt_ref[...]) * y


def _attn_layer(x, mod, nrm, w_q_t, qn_col, cos_t, sin_t, k, v_t, sink_row, w_o, tq):
    bsz, seq, d = x.shape
    nk = N_KV_HEADS * HEAD_DIM
    half = HEAD_DIM // 2
    nqb = tq // BLOCK
    row = lambda kk: pl.BlockSpec((None, 1, d), lambda b, s, kk=kk: (b, 0, kk))
    full = lambda a: pl.BlockSpec(a.shape, lambda b, s, n=a.ndim: (0,) * n)
    tile = pl.BlockSpec((None, tq, d), lambda b, s: (b, s, 0))
    rope = pl.BlockSpec((None, half, tq), lambda b, s: (b, 0, s))
    prev = lambda s: jnp.maximum(s * nqb - 1, 0)
    return pl.pallas_call(
        _attn_kernel,
        grid=(bsz, seq // tq),
        in_specs=[tile, row(0), row(1), row(2), full(nrm), full(w_q_t), full(qn_col), rope, rope,
                  pl.BlockSpec((None, tq, nk), lambda b, s: (b, s, 0)),
                  pl.BlockSpec((None, BLOCK, nk), lambda b, s: (b, prev(s), 0)),
                  pl.BlockSpec((None, nk, tq), lambda b, s: (b, 0, s)),
                  pl.BlockSpec((None, nk, BLOCK), lambda b, s: (b, 0, prev(s))),
                  full(sink_row), full(w_o)],
        out_specs=tile,
        out_shape=jax.ShapeDtypeStruct(x.shape, F32),
        scratch_shapes=[pltpu.VMEM((d, tq), F32), pltpu.VMEM((d, tq), BF16)],
        compiler_params=_cparams("parallel", "parallel"),
        name="attn_layer",
    )(x, mod, mod, mod, nrm, w_q_t, qn_col, cos_t, sin_t, k, k, v_t, v_t, sink_row, w_o)


def _ffn_kernel(x_ref, sh_ref, sc_ref, gt_ref, nrm_ref, wg_ref, wu_ref, wd_ref, o_ref, h_s, acc):
    j = pl.program_id(1)

    @pl.when(j == 0)
    def _():
        h_s[...] = _modulate(x_ref[...], nrm_ref[...], sh_ref[...], sc_ref[...]).astype(BF16)
        acc[...] = jnp.zeros_like(acc)

    h = h_s[...]
    g = jnp.dot(h, wg_ref[...], preferred_element_type=F32)
    u = jnp.dot(h, wu_ref[...], preferred_element_type=F32)
    acc[...] += jnp.dot((_silu(g) * u).astype(BF16), wd_ref[...], preferred_element_type=F32)

    @pl.when(j == pl.num_programs(1) - 1)
    def _():
        o_ref[...] = x_ref[...] + (1.0 + gt_ref[...]) * acc[...]


def _ffn_layer(x, mod, nrm, w_gu, w_down, layer, tm, tf):
    bsz, seq, d = x.shape
    ff = w_down.shape[1]
    nt = seq // tm
    nj = ff // tf
    row = lambda k: pl.BlockSpec((None, 1, d), lambda i, j, k=k: (i // nt, 0, k))
    tile = pl.BlockSpec((None, tm, d), lambda i, j: (i // nt, i % nt, 0))
    return pl.pallas_call(
        _ffn_kernel,
        grid=(bsz * nt, nj),
        in_specs=[tile, row(3), row(4), row(5), pl.BlockSpec(nrm.shape, lambda i, j: (0, 0)),
                  pl.BlockSpec((None, d, tf), lambda i, j: (layer, 0, j)),
                  pl.BlockSpec((None, d, tf), lambda i, j: (layer, 0, j + nj)),
                  pl.BlockSpec((None, tf, d), lambda i, j: (layer, j, 0))],
        out_specs=tile,
        out_shape=jax.ShapeDtypeStruct(x.shape, F32),
        scratch_shapes=[pltpu.VMEM((tm, d), BF16), pltpu.VMEM((tm, d), F32)],
        compiler_params=_cparams("parallel", "arbitrary"),
        name="ffn_layer",
    )(x, mod, mod, mod, nrm, w_gu, w_gu, w_down)


def _top2(logits, n_experts):
    lane = lax.broadcasted_iota(jnp.int32, logits.shape, 1)
    logits = jnp.where(lane < n_experts, logits, -jnp.inf)
    m1 = jnp.max(logits, axis=1, keepdims=True)
    i1 = jnp.min(jnp.where(logits == m1, lane, LANES), axis=1, keepdims=True)
    rest = jnp.where(lane == i1, -jnp.inf, logits)
    m2 = jnp.max(rest, axis=1, keepdims=True)
    i2 = jnp.min(jnp.where(rest == m2, lane, LANES), axis=1, keepdims=True)
    w1 = 1.0 / (1.0 + jnp.exp(m2 - m1))
    return lane, i1, i2, w1


def _router_logits(h, rhi_ref, rlo_ref):
    h_hi = h.astype(BF16)
    h_lo = (h - h_hi.astype(F32)).astype(BF16)
    return (jnp.dot(h_hi, rhi_ref[...], preferred_element_type=F32)
            + jnp.dot(h_lo, rhi_ref[...], preferred_element_type=F32)
            + jnp.dot(h_hi, rlo_ref[...], preferred_element_type=F32))


_M_E1, _M_E2, _M_R1, _M_R2, _M_W1, _M_W2 = range(6)


def _router_kernel(n_experts, x_ref, sh_ref, sc_ref, nrm_ref, rhi_ref, rlo_ref,
                   h_ref, meta_ref, cnt_ref, carry):
    tr = x_ref.shape[0]

    @pl.when(pl.program_id(1) == 0)
    def _():
        carry[...] = jnp.zeros_like(carry)

    h = _modulate(x_ref[...], nrm_ref[...], sh_ref[...], sc_ref[...])
    h_ref[...] = h.astype(BF16)
    lane, i1, i2, w1 = _top2(_router_logits(h, rhi_ref, rlo_ref), n_experts)
    sel = (lane == i1) | (lane == i2)
    r = lax.broadcasted_iota(jnp.int32, (tr, tr), 0)
    cc = lax.broadcasted_iota(jnp.int32, (tr, tr), 1)
    ltri = jnp.where(cc < r, 1.0, 0.0).astype(BF16)
    onehot = jnp.where(sel, 1.0, 0.0)
    rank_all = jnp.dot(ltri, onehot.astype(BF16), preferred_element_type=F32) + carry[...]
    rank1 = jnp.sum(jnp.where(lane == i1, rank_all, 0.0), axis=1, keepdims=True)
    rank2 = jnp.sum(jnp.where(lane == i2, rank_all, 0.0), axis=1, keepdims=True)
    meta = jnp.zeros(lane.shape, F32)
    for idx, val in ((_M_E1, i1.astype(F32)), (_M_E2, i2.astype(F32)), (_M_R1, rank1), (_M_R2, rank2),
                     (_M_W1, w1), (_M_W2, 1.0 - w1)):
        meta = jnp.where(lane == idx, val, meta)
    meta_ref[...] = meta
    carry[...] += jnp.sum(onehot, axis=0, keepdims=True)
    cnt_ref[...] = carry[...]


def _moe_router(x, mod, nrm, r_hi, r_lo, n_experts, nb, tr):
    bsz, seq, d = x.shape
    t = bsz * seq
    nblk, nst = t // nb, nb // tr
    per_seq = seq // nb
    x2 = x.reshape(t, d)
    row = lambda k: pl.BlockSpec((None, 1, d), lambda b, s, k=k: (b // per_seq, 0, k))
    const = lambda a: pl.BlockSpec(a.shape, lambda b, s, n=a.ndim: (0,) * n)
    tile = lambda w: pl.BlockSpec((tr, w), lambda b, s: (b * nst + s, 0))
    return pl.pallas_call(
        functools.partial(_router_kernel, n_experts),
        grid=(nblk, nst),
        in_specs=[tile(d), row(3), row(4), const(nrm), const(r_hi), const(r_lo)],
        out_specs=[tile(d), tile(LANES), pl.BlockSpec((None, 1, LANES), lambda b, s: (b, 0, 0))],
        out_shape=[jax.ShapeDtypeStruct((t, d), BF16), jax.ShapeDtypeStruct((t, LANES), F32),
                   jax.ShapeDtypeStruct((nblk, 1, LANES), F32)],
        scratch_shapes=[pltpu.VMEM((1, LANES), F32)],
        compiler_params=_cparams("parallel", "arbitrary"),
        name="moe_router",
    )(x2, mod, mod, nrm, r_hi, r_lo)


def _row_slab(ref, r):
    return ref.at[pl.ds(pl.multiple_of(r * SUBLANES, SUBLANES), SUBLANES), :]


def _run_copies(n_experts, b, cnt_ref, goff_ref, hbm, stage, sem, to_hbm, wait):
    def copy(loc_row, glob_row, rows):
        loc = stage.at[pl.ds(pl.multiple_of(loc_row, ROW_ALIGN), rows), :]
        glob = hbm.at[pl.ds(pl.multiple_of(glob_row, ROW_ALIGN), rows), :]
        dma = pltpu.make_async_copy(loc, glob, sem) if to_hbm else pltpu.make_async_copy(glob, loc, sem)
        if wait:
            dma.wait()
        else:
            dma.start()

    base = jnp.int32(0)
    for e in range(n_experts):
        n = cnt_ref[b * n_experts + e]
        g = goff_ref[b * n_experts + e]
        nfull = n // RUN_CHUNK
        rem = n - nfull * RUN_CHUNK

        def full(ci, carry, base=base, g=g):
            copy(base + ci * RUN_CHUNK, g + ci * RUN_CHUNK, RUN_CHUNK)
            return carry

        lax.fori_loop(0, nfull, full, 0)
        piece = RUN_CHUNK // 2
        while piece >= ROW_ALIGN:
            off = nfull * RUN_CHUNK + rem - rem % (2 * piece)

            @pl.when(rem % (2 * piece) >= piece)
            def _(base=base, g=g, off=off, piece=piece):
                copy(base + off, g + off, piece)

            piece //= 2
        base = base + n


def _dispatch_kernel(n_experts, cnt_ref, goff_ref, h_ref, dest_ref, xs_in, xs_hbm,
                     hslab, xslab, stage, sem):
    del xs_in
    b, st = pl.program_id(0), pl.program_id(1)
    tr, d = h_ref.shape
    nsl = d // LANES
    cap = stage.shape[0]

    @pl.when((b == 0) & (st == 0))
    def _():
        xslab[...] = jnp.zeros_like(xslab)

    hf = h_ref[...].astype(F32)
    for j in range(nsl):
        hslab[pl.ds(j, tr, stride=nsl), :] = hf[:, j * LANES:(j + 1) * LANES]

    def scatter(i, carry):
        for u in range(TOK_UNROLL):
            t = i * TOK_UNROLL + u
            slab = _row_slab(hslab, t)[...]
            _row_slab(xslab, dest_ref[0, 2 * t])[...] = slab
            _row_slab(xslab, dest_ref[0, 2 * t + 1])[...] = slab
        return carry

    lax.fori_loop(0, tr // TOK_UNROLL, scatter, 0)

    @pl.when(st == pl.num_programs(1) - 1)
    def _():
        for c in range(cap // RUN_CHUNK):
            r0 = c * RUN_CHUNK
            tile = jnp.concatenate([xslab[pl.ds(r0 * nsl + j, RUN_CHUNK, stride=nsl), :] for j in range(nsl)],
                                   axis=1)
            stage[r0:r0 + RUN_CHUNK, :] = tile.astype(BF16)
        _run_copies(n_experts, b, cnt_ref, goff_ref, xs_hbm, stage, sem, True, False)
        _run_copies(n_experts, b, cnt_ref, goff_ref, xs_hbm, stage, sem, True, True)


def _moe_dispatch(h, dest, cnt_pad, run_goff, n_rows, n_experts, nb, tr, cap):
    t, d = h.shape
    nblk, nst = t // nb, nb // tr
    xs0 = jnp.zeros((n_rows, d), BF16)
    grid_spec = pltpu.PrefetchScalarGridSpec(
        num_scalar_prefetch=2,
        grid=(nblk, nst),
        in_specs=[pl.BlockSpec((tr, d), lambda b, s, *_: (b * nst + s, 0)),
                  pl.BlockSpec((None, None, 1, 2 * tr), lambda b, s, *_: (b, s, 0, 0), memory_space=pltpu.SMEM),
                  pl.BlockSpec(memory_space=pl.ANY)],
        out_specs=pl.BlockSpec(memory_space=pl.ANY),
        scratch_shapes=[pltpu.VMEM((tr * (d // LANES), LANES), F32),
                        pltpu.VMEM((cap * (d // LANES), LANES), F32),
                        pltpu.VMEM((cap, d), BF16),
                        pltpu.SemaphoreType.DMA(())],
    )
    return pl.pallas_call(
        functools.partial(_dispatch_kernel, n_experts),
        grid_spec=grid_spec,
        out_shape=jax.ShapeDtypeStruct((n_rows, d), BF16),
        input_output_aliases={4: 0},
        compiler_params=_cparams("arbitrary", "arbitrary"),
        name="moe_dispatch",
    )(cnt_pad, run_goff, h, dest, xs0)


def _gmm_kernel(te_ref, xrow_ref, valid_ref, xs_ref, wg_ref, wu_ref, wd_ref, ys_ref, acc):
    del te_ref, xrow_ref
    i, j = pl.program_id(0), pl.program_id(1)

    @pl.when(valid_ref[i] != 0)
    def _():
        @pl.when(j == 0)
        def _():
            acc[...] = jnp.zeros_like(acc)

        h = xs_ref[...]
        g = jnp.dot(h, wg_ref[...], preferred_element_type=F32)
        u = jnp.dot(h, wu_ref[...], preferred_element_type=F32)
        acc[...] += jnp.dot((_silu(g) * u).astype(BF16), wd_ref[...], preferred_element_type=F32)

        @pl.when(j == pl.num_programs(1) - 1)
        def _():
            ys_ref[...] = acc[...].astype(BF16)

    @pl.when((valid_ref[i] == 0) & (j == 0))
    def _():
        ys_ref[...] = jnp.zeros_like(ys_ref)


def _moe_experts(xs, tile_expert, tile_row, tile_valid, w_gu, w_down, layer, tm, tf):
    n_rows, d = xs.shape
    ffe = w_down.shape[2]
    nj = ffe // tf
    grid_spec = pltpu.PrefetchScalarGridSpec(
        num_scalar_prefetch=3,
        grid=(n_rows // tm, nj),
        in_specs=[pl.BlockSpec((tm, d), lambda i, j, te, xr, va: (xr[i], 0)),
                  pl.BlockSpec((None, None, d, tf), lambda i, j, te, xr, va: (layer, te[i], 0, j)),
                  pl.BlockSpec((None, None, d, tf), lambda i, j, te, xr, va: (layer, te[i], 0, j + nj)),
                  pl.BlockSpec((None, None, tf, d), lambda i, j, te, xr, va: (layer, te[i], j, 0))],
        out_specs=pl.BlockSpec((tm, d), lambda i, j, te, xr, va: (i, 0)),
        scratch_shapes=[pltpu.VMEM((tm, d), F32)],
    )
    return pl.pallas_call(
        _gmm_kernel,
        grid_spec=grid_spec,
        out_shape=jax.ShapeDtypeStruct((n_rows, d), BF16),
        compiler_params=_cparams("arbitrary", "arbitrary"),
        name="moe_experts",
    )(tile_expert, tile_row, tile_valid, xs, w_gu, w_gu, w_down)


def _combine_kernel(n_experts, cnt_ref, goff_ref, x_ref, gt_ref, dest_ref, w_ref, ys_hbm, o_ref,
                    yslab, stage, mslab, sem):
    b, st = pl.program_id(0), pl.program_id(1)
    tr, d = x_ref.shape
    nsl = d // LANES
    cap = stage.shape[0]

    @pl.when((b == 0) & (st == 0))
    def _():
        stage[...] = jnp.zeros_like(stage)

    @pl.when(st == 0)
    def _():
        _run_copies(n_experts, b, cnt_ref, goff_ref, ys_hbm, stage, sem, False, False)
        _run_copies(n_experts, b, cnt_ref, goff_ref, ys_hbm, stage, sem, False, True)
        for c in range(cap // RUN_CHUNK):
            r0 = c * RUN_CHUNK
            tile = stage[r0:r0 + RUN_CHUNK, :].astype(F32)
            for j in range(nsl):
                yslab[pl.ds(r0 * nsl + j, RUN_CHUNK, stride=nsl), :] = tile[:, j * LANES:(j + 1) * LANES]

    def gather(i, carry):
        for u in range(TOK_UNROLL):
            t = i * TOK_UNROLL + u
            y1 = _row_slab(yslab, dest_ref[0, 2 * t])[...]
            y2 = _row_slab(yslab, dest_ref[0, 2 * t + 1])[...]
            _row_slab(mslab, t)[...] = w_ref[0, 2 * t] * y1 + w_ref[0, 2 * t + 1] * y2
        return carry

    lax.fori_loop(0, tr // TOK_UNROLL, gather, 0)
    moe = jnp.concatenate([mslab[pl.ds(j, tr, stride=nsl), :] for j in range(nsl)], axis=1)
    o_ref[...] = x_ref[...] + (1.0 + gt_ref[...]) * moe


def _moe_combine(x, mod, dest, wts, ys, cnt_pad, run_goff, n_experts, nb, tr, cap):
    bsz, seq, d = x.shape
    t = bsz * seq
    nblk, nst = t // nb, nb // tr
    per_seq = seq // nb
    smem = lambda: pl.BlockSpec((None, None, 1, 2 * tr), lambda b, s, *_: (b, s, 0, 0), memory_space=pltpu.SMEM)
    tile = pl.BlockSpec((tr, d), lambda b, s, *_: (b * nst + s, 0))
    grid_spec = pltpu.PrefetchScalarGridSpec(
        num_scalar_prefetch=2,
        grid=(nblk, nst),
        in_specs=[tile, pl.BlockSpec((None, 1, d), lambda b, s, *_: (b // per_seq, 0, 5)), smem(), smem(),
                  pl.BlockSpec(memory_space=pl.ANY)],
        out_specs=tile,
        scratch_shapes=[pltpu.VMEM((cap * (d // LANES), LANES), F32),
                        pltpu.VMEM((cap, d), BF16),
                        pltpu.VMEM((tr * (d // LANES), LANES), F32),
                        pltpu.SemaphoreType.DMA(())],
    )
    out = pl.pallas_call(
        functools.partial(_combine_kernel, n_experts),
        grid_spec=grid_spec,
        out_shape=jax.ShapeDtypeStruct((t, d), F32),
        compiler_params=_cparams("arbitrary", "arbitrary"),
        name="moe_combine",
    )(cnt_pad, run_goff, x.reshape(t, d), mod, dest, wts, ys)
    return out.reshape(bsz, seq, d)


def _round_up(v, m):
    return (v + m - 1) // m * m


def _moe_layer(x, mod, nrm, router, w_gu, w_down, layer, nb, tr, tm, tf):
    bsz, seq, d = x.shape
    t = bsz * seq
    n_experts = router.shape[-1]
    nblk, nst = t // nb, nb // tr
    r_pad = jnp.pad(router, ((0, 0), (0, LANES - n_experts)))
    r_hi = r_pad.astype(BF16)
    r_lo = (r_pad - r_hi.astype(F32)).astype(BF16)
    h, meta, cnt = _moe_router(x, mod, nrm, r_hi, r_lo, n_experts, nb, tr)

    cnt = cnt[:, 0, :n_experts].astype(jnp.int32)
    cnt_pad = _round_up(cnt, ROW_ALIGN)
    local_off = jnp.cumsum(cnt_pad, axis=1) - cnt_pad
    region = _round_up(jnp.sum(cnt_pad, axis=0), tm)
    region_end = jnp.cumsum(region)
    run_goff = (region_end - region)[None, :] + jnp.cumsum(cnt_pad, axis=0) - cnt_pad
    n_rows = _round_up(TOP_K * t + nblk * n_experts * ROW_ALIGN + n_experts * tm, tm)
    cap = _round_up(TOP_K * nb + n_experts * ROW_ALIGN, RUN_CHUNK)
    tile_start = jnp.arange(n_rows // tm, dtype=jnp.int32) * tm
    tile_expert = jnp.minimum(jnp.sum(tile_start[:, None] >= region_end[None, :], axis=1), n_experts - 1)
    tile_valid = (tile_start < region_end[-1]).astype(jnp.int32)
    tile_row = jnp.minimum(tile_start // tm, region_end[-1] // tm - 1)
    experts = meta[:, _M_E1:_M_E2 + 1].astype(jnp.int32)
    ranks = meta[:, _M_R1:_M_R2 + 1].astype(jnp.int32)
    onehot = experts[:, :, None] == jnp.arange(n_experts, dtype=jnp.int32)
    tok_off = jnp.repeat(local_off, nb, axis=0)
    dest = (jnp.sum(jnp.where(onehot, tok_off[:, None, :], 0), axis=-1) + ranks).reshape(nblk, nst, 1, 2 * tr)
    wts = meta[:, _M_W1:_M_W2 + 1].reshape(nblk, nst, 1, 2 * tr)
    cnt_flat, goff_flat = cnt_pad.reshape(-1), run_goff.reshape(-1).astype(jnp.int32)

    xs = _moe_dispatch(h, dest, cnt_flat, goff_flat, n_rows, n_experts, nb, tr, cap)
    ys = _moe_experts(xs, tile_expert.astype(jnp.int32), tile_row.astype(jnp.int32), tile_valid, w_gu, w_down, layer, tm, tf)
    return _moe_combine(x, mod, dest, wts, ys, cnt_flat, goff_flat, n_experts, nb, tr, cap)


def _pick(n, pref):
    return pref if n % pref == 0 else n


def kernel(x, c, positions, ada_w, ada_b, mix_norm, ffn_norm, a_w_in, a_conv_w, a_conv_b, a_gx_w, a_gx_b, a_ga_w, a_ga_b, a_lambda, a_w_out, kv_norm, kv_ada_w, kv_ada_b, w_kv, k_norm, b_w_q, b_q_norm, b_sinks, b_w_o, f_w_gu, f_w_down, m_router, m_w_gu, m_w_down):
    bsz, seq, d = x.shape
    depth = ada_w.shape[0]
    n_a = a_w_in.shape[0]
    ts = _pick(seq, 512)
    tm = _pick(seq, 1024)
    tf = 512
    nb = _pick(seq, 2048)
    tr = _pick(nb, 512)

    mods = _cond_linear(c, ada_w, ada_b).reshape(depth, bsz, 1, 6 * d)
    kvmod = _cond_linear(c, kv_ada_w[None], kv_ada_b[None]).reshape(bsz, 1, 2 * d)
    cos_t, sin_t = _rope_tables(positions)

    row = lambda v: v.reshape(1, -1)
    col = lambda v, n: jnp.broadcast_to(v.astype(F32)[:, None], (v.shape[0], n))
    k_arr = vt_arr = None
    f_gu, f_down = f_w_gu.astype(BF16), f_w_down.astype(BF16)
    m_gu, m_down = m_w_gu.astype(BF16), m_w_down.astype(BF16)
    for l in range(depth):
        mod = mods[l]
        if l < n_a:
            wg = jnp.concatenate([a_ga_w[l], a_gx_w[l]], axis=-1).astype(BF16)
            x = _rglru_layer(x, mod, row(mix_norm[l]), a_w_in[l].astype(BF16), a_conv_w[l], row(a_conv_b[l]),
                             wg, row(a_ga_b[l]), row(a_gx_b[l]), row(a_lambda[l]), a_w_out[l].astype(BF16), ts)
        else:
            if l == n_a:
                k_arr, vt_arr = _shared_kv(x, kvmod, row(kv_norm), w_kv.T.astype(BF16), col(k_norm, ts),
                                           cos_t, sin_t, ts)
            bi = l - n_a
            sink_row = jnp.repeat(b_sinks[bi].astype(F32) * LOG2E, BLOCK).reshape(1, -1)
            q_gain = b_q_norm[bi].astype(F32) * (HEAD_DIM ** -0.5 * LOG2E)
            x = _attn_layer(x, mod, row(mix_norm[l]), b_w_q[bi].T.astype(BF16), col(q_gain, ts),
                            cos_t, sin_t, k_arr, vt_arr, sink_row, b_w_o[bi].astype(BF16), ts)
        if l % 2 == 0:
            x = _ffn_layer(x, mod, row(ffn_norm[l]), f_gu, f_down, l // 2, tm, tf)
        else:
            x = _moe_layer(x, mod, row(ffn_norm[l]), m_router[l // 2], m_gu, m_down, l // 2, nb, tr, tm, tf)
    return x
```
